```python
import jax, jax.numpy as jnp
from jax import lax
import numpy as np

D_MODEL = 4096
BATCH = 1
SEQ = 8192
DEPTH = 1

CHUNK = 64
Q_BLOCK = 128
N_MEM = 256
SB_HEADS = 12
SB_HEAD_DIM = 128
SB_W = SB_HEADS * SB_HEAD_DIM
MLA_HEADS = 12
Q_LORA = 896
KV_LORA = 512
NOPE_DIM = 128
ROPE_DIM = 64
V_DIM = 128
QK_DIM = NOPE_DIM + ROPE_DIM
MLA_W = MLA_HEADS * V_DIM
MEM_HEADS = 4
MEM_HEAD_DIM = 256
MEM_W = MEM_HEADS * MEM_HEAD_DIM
N_BRANCH = 3
D_FF = 4 * D_MODEL
ROPE_THETA = 10000.0
EPS = 1e-6
IN_SIZES = (SB_W, SB_W, SB_W, Q_LORA, KV_LORA, ROPE_DIM, MEM_W, N_BRANCH * D_MODEL)
N_IN = SB_W * 3 + Q_LORA + KV_LORA + ROPE_DIM + MEM_W + N_BRANCH * D_MODEL

kernel_name = "chunk_causal_hybrid_sb_mla_mem_block"


def rmsnorm(x, g):
    xf = x.astype(jnp.float32)
    y = xf * lax.rsqrt(jnp.mean(xf * xf, axis=-1, keepdims=True) + EPS)
    return (y * g.astype(jnp.float32)).astype(x.dtype)


def to_heads(t, n_heads):
    b, s, _ = t.shape
    return t.reshape(b, s, n_heads, -1).transpose(0, 2, 1, 3)


def merge_heads(t):
    b, h, s, d = t.shape
    return t.transpose(0, 2, 1, 3).reshape(b, s, h * d)


def rope(t, positions):
    half = t.shape[-1] // 2
    freqs = 1.0 / (ROPE_THETA ** (jnp.arange(half, dtype=jnp.float32) / half))
    ang = positions.astype(jnp.float32)[..., None] * freqs
    if t.ndim == 4:
        ang = ang[:, None]
    cos, sin = jnp.cos(ang), jnp.sin(ang)
    tf = t.astype(jnp.float32)
    t1, t2 = tf[..., :half], tf[..., half:]
    return jnp.concatenate([t1 * cos - t2 * sin, t2 * cos + t1 * sin], axis=-1).astype(t.dtype)


def sweep_query_blocks(block_fn, q):
    b, h, s, d = q.shape
    nb = s // Q_BLOCK
    qb = q.reshape(b, h, nb, Q_BLOCK, d).transpose(2, 0, 1, 3, 4)
    starts = jnp.arange(nb, dtype=jnp.int32) * Q_BLOCK
    out = lax.map(block_fn, (qb, starts))
    return out.transpose(1, 2, 0, 3, 4).reshape(b, h, s, -1)


def stick_breaking_attention(q, k, v):
    s_len = k.shape[2]
    scale = 1.0 / np.sqrt(q.shape[-1]).astype(np.float32)
    key_pos = jnp.arange(s_len, dtype=jnp.int32)

    def block(args):
        qb, t0 = args
        z = jnp.einsum('bhqd,bhkd->bhqk', qb, k).astype(jnp.float32) * scale
        t = t0 + jnp.arange(Q_BLOCK, dtype=jnp.int32)
        causal = key_pos[None, :] < t[:, None]
        l = jnp.where(causal, jax.nn.log_sigmoid(-z), 0.0)
        rev = lax.cumsum(l, axis=3, reverse=True)
        log_a = jax.nn.log_sigmoid(z) + rev - l
        a = jnp.where(causal, jnp.exp(log_a), 0.0)
        return jnp.einsum('bhqk,bhkd->bhqd', a.astype(v.dtype), v)

    return sweep_query_blocks(block, q)


def chunk_causal_softmax_attention(q, k, v):
    s_len = k.shape[2]
    scale = 1.0 / np.sqrt(q.shape[-1]).astype(np.float32)
    key_chunk = jnp.arange(s_len, dtype=jnp.int32) // CHUNK

    def block(args):
        qb, t0 = args
        sc = jnp.einsum('bhqd,bhkd->bhqk', qb, k).astype(jnp.float32) * scale
        q_chunk = (t0 + jnp.arange(Q_BLOCK, dtype=jnp.int32)) // CHUNK
        mask = key_chunk[None, :] <= q_chunk[:, None]
        p = jax.nn.softmax(jnp.where(mask, sc, -jnp.inf), axis=-1)
        return jnp.einsum('bhqk,bhkd->bhqd', p.astype(v.dtype), v)

    return sweep_query_blocks(block, q)


def hybrid_layer(x, mem, positions, g_mix, g_mem, w_in, g_cq, g_ckv, w_q_b, w_kv_b,
                 g_q_mla, g_k_mla, w_mem_kv, g_q_mem, g_k_mem, w_sb_o, w_mla_o, w_mem_o,
                 w_out, g_ffn, w_ff1, w_ff2):
    b, s, _ = x.shape
    h = rmsnorm(x, g_mix)
    proj = h @ w_in
    cuts, acc = [], 0
    for n in IN_SIZES[:-1]:
        acc += n
        cuts.append(acc)
    sb_q, sb_k, sb_v, c_q, c_kv, k_pe, mem_q, gate_logits = jnp.split(proj, cuts, axis=-1)

    o_sb = merge_heads(stick_breaking_attention(to_heads(sb_q, SB_HEADS),
                                                to_heads(sb_k, SB_HEADS),
                                                to_heads(sb_v, SB_HEADS)))

    q = to_heads(rmsnorm(c_q, g_cq) @ w_q_b, MLA_HEADS)
    q = jnp.concatenate([q[..., :NOPE_DIM], rope(q[..., NOPE_DIM:], positions)], axis=-1)
    kv = to_heads(rmsnorm(c_kv, g_ckv) @ w_kv_b, MLA_HEADS)
    k_nope, v_mla = kv[..., :NOPE_DIM], kv[..., NOPE_DIM:]
    k_rot = jnp.broadcast_to(rope(k_pe, positions)[:, None], (b, MLA_HEADS, s, ROPE_DIM))
    k = jnp.concatenate([k_nope, k_rot], axis=-1)
    q, k = rmsnorm(q, g_q_mla), rmsnorm(k, g_k_mla)
    o_mla = merge_heads(chunk_causal_softmax_attention(q, k, v_mla))

    mkv = rmsnorm(mem, g_mem) @ w_mem_kv
    mk, mv = to_heads(mkv[..., :MEM_W], MEM_HEADS), to_heads(mkv[..., MEM_W:], MEM_HEADS)
    mq = rmsnorm(to_heads(mem_q, MEM_HEADS), g_q_mem)
    mk = rmsnorm(mk, g_k_mem)
    sc = jnp.einsum('bhqd,bhkd->bhqk', mq, mk).astype(jnp.float32) / np.float32(np.sqrt(MEM_HEAD_DIM))
    p = jax.nn.softmax(sc, axis=-1)
    o_mem = merge_heads(jnp.einsum('bhqk,bhkd->bhqd', p.astype(mv.dtype), mv))

    gates = jax.nn.sigmoid(gate_logits.astype(jnp.float32)).reshape(b, s, N_BRANCH, D_MODEL)
    merged = (gates[:, :, 0] * (o_sb @ w_sb_o).astype(jnp.float32)
              + gates[:, :, 1] * (o_mla @ w_mla_o).astype(jnp.float32)
              + gates[:, :, 2] * (o_mem @ w_mem_o).astype(jnp.float32)).astype(x.dtype)
    x = x + merged @ w_out

    u = jax.nn.relu(rmsnorm(x, g_ffn) @ w_ff1)
    return x + (u * u) @ w_ff2


def setup_inputs(seed: int = 0) -> dict:
    key = jax.random.key(seed)
    ks = jax.random.split(key, 24)
    f32 = jnp.float32

    def w(k, shape, fan_in):
        return jax.random.normal(k, (DEPTH,) + shape, f32) * (fan_in ** -0.5)

    def gain(k, n):
        return 1.0 + 0.05 * jax.random.normal(k, (DEPTH, n), f32)

    x = jax.random.normal(ks[0], (BATCH, SEQ, D_MODEL), f32)
    mem = jax.random.normal(ks[1], (BATCH, N_MEM, D_MODEL), f32)
    start = jax.random.randint(ks[2], (BATCH, 1), 0, 1024, dtype=jnp.int32)
    positions = start + jnp.arange(SEQ, dtype=jnp.int32)[None, :]
    return {
        "x": x,
        "mem": mem,
        "positions": positions,
        "g_mix": gain(ks[3], D_MODEL),
        "g_mem": gain(ks[4], D_MODEL),
        "w_in": w(ks[5], (D_MODEL, N_IN), D_MODEL),
        "g_cq": gain(ks[6], Q_LORA),
        "g_ckv": gain(ks[7], KV_LORA),
        "w_q_b": w(ks[8], (Q_LORA, MLA_HEADS * QK_DIM), Q_LORA),
        "w_kv_b": w(ks[9], (KV_LORA, MLA_HEADS * (NOPE_DIM + V_DIM)), KV_LORA),
        "g_q_mla": gain(ks[10], QK_DIM),
        "g_k_mla": gain(ks[11], QK_DIM),
        "w_mem_kv": w(ks[12], (D_MODEL, 2 * MEM_W), D_MODEL),
        "g_q_mem": gain(ks[13], MEM_HEAD_DIM),
        "g_k_mem": gain(ks[14], MEM_HEAD_DIM),
        "w_sb_o": w(ks[15], (SB_W, D_MODEL), SB_W),
        "w_mla_o": w(ks[16], (MLA_W, D_MODEL), MLA_W),
        "w_mem_o": w(ks[17], (MEM_W, D_MODEL), MEM_W),
        "w_out": w(ks[18], (D_MODEL, D_MODEL), D_MODEL),
        "g_ffn": gain(ks[19], D_MODEL),
        "w_ff1": w(ks[20], (D_MODEL, D_FF), D_MODEL),
        "w_ff2": w(ks[21], (D_FF, D_MODEL), D_FF),
    }


def reference(x, mem, positions, g_mix, g_mem, w_in, g_cq, g_ckv, w_q_b, w_kv_b,
              g_q_mla, g_k_mla, w_mem_kv, g_q_mem, g_k_mem, w_sb_o, w_mla_o, w_mem_o,
              w_out, g_ffn, w_ff1, w_ff2):
    for i in range(DEPTH):
        x = hybrid_layer(x, mem, positions, g_mix[i], g_mem[i], w_in[i], g_cq[i], g_ckv[i],
                         w_q_b[i], w_kv_b[i], g_q_mla[i], g_k_mla[i], w_mem_kv[i],
                         g_q_mem[i], g_k_mem[i], w_sb_o[i], w_mla_o[i], w_mem_o[i],
                         w_out[i], g_ffn[i], w_ff1[i], w_ff2[i])
    return x
```

```python
import functools

import jax
import jax.numpy as jnp
import numpy as np
from jax import lax
from jax.experimental import pallas as pl
from jax.experimental.pallas import tpu as pltpu

F32 = jnp.float32
BF16 = jnp.bfloat16

CHUNK = 64
SB_HEADS = 12
SB_HEAD_DIM = 128
SB_W = SB_HEADS * SB_HEAD_DIM
MLA_HEADS = 12
Q_LORA = 896
KV_LORA = 512
NOPE_DIM = 128
ROPE_DIM = 64
V_DIM = 128
QK_DIM = NOPE_DIM + ROPE_DIM
MEM_HEADS = 4
MEM_HEAD_DIM = 256
MEM_W = MEM_HEADS * MEM_HEAD_DIM
ROPE_THETA = 10000.0
EPS = 1e-6

LANES = 128
VMEM_LIMIT_BYTES = 56 * 1024 * 1024
SB_DEAD_LOG = -120.0
ATTN_TILE = 256


def _params(*sem):
    return pltpu.CompilerParams(dimension_semantics=sem, vmem_limit_bytes=VMEM_LIMIT_BYTES)


def _tile(n, pref):
    t = min(n, pref)
    assert n % t == 0, (n, t)
    return t


def _rmsnorm_kernel(x_ref, g_ref, o_ref):
    x = x_ref[...]
    ms = jnp.mean(x * x, axis=-1, keepdims=True)
    o_ref[...] = (x * lax.rsqrt(ms + EPS) * g_ref[...]).astype(o_ref.dtype)


def _rmsnorm(x, g, name):
    m, d = x.shape
    tm = _tile(m, 256)
    return pl.pallas_call(
        _rmsnorm_kernel,
        grid=(m // tm,),
        in_specs=[pl.BlockSpec((tm, d), lambda i: (i, 0)), pl.BlockSpec((1, d), lambda i: (0, 0))],
        out_specs=pl.BlockSpec((tm, d), lambda i: (i, 0)),
        out_shape=jax.ShapeDtypeStruct((m, d), BF16),
        compiler_params=_params("parallel"),
        name=name,
    )(x, g.reshape(1, d))


def _mm_kernel(epilogue, n_extra, a_ref, b_ref, *refs):
    acc = jnp.dot(a_ref[...], b_ref[...], preferred_element_type=F32)
    epilogue(acc, refs[:n_extra], refs[n_extra:])


def _matmul(a, b, *, tm, tn, extra=(), outs, epilogue, name):
    m, k = a.shape
    k2, n = b.shape
    assert k == k2 and m % tm == 0 and n % tn == 0
    in_specs = [pl.BlockSpec((tm, k), lambda i, j: (i, 0)), pl.BlockSpec((k, tn), lambda i, j: (0, j))]
    in_specs += [pl.BlockSpec(bs, im) for _, bs, im in extra]
    out_specs = [pl.BlockSpec(bs, im) for _, _, bs, im in outs]
    out_shape = [jax.ShapeDtypeStruct(s, dt) for s, dt, _, _ in outs]
    res = pl.pallas_call(
        functools.partial(_mm_kernel, epilogue, len(extra)),
        grid=(m // tm, n // tn),
        in_specs=in_specs,
        out_specs=out_specs,
        out_shape=out_shape,
        compiler_params=_params("parallel", "parallel"),
        name=name,
    )(a, b, *[x for x, _, _ in extra])
    return res


def _sumsq(x):
    return jnp.sum(x * x, axis=-1, keepdims=True)


def _ep_colscale(acc, extra, outs):
    (scale_ref,) = extra
    (o_ref,) = outs
    o_ref[...] = (acc * scale_ref[...]).astype(o_ref.dtype)


def _ep_sigmoid(acc, extra, outs):
    (o_ref,) = outs
    o_ref[...] = (1.0 / (1.0 + jnp.exp(-acc))).astype(o_ref.dtype)


def _ep_latent(acc, extra, outs):
    pos_ref, freq_ref, cmask_ref, smask_ref, gq_ref, gkv_ref = extra
    cq_ref, ckv_ref, krot_ref, cos_ref, sin_ref = outs
    cq = acc[:, :Q_LORA]
    inv = lax.rsqrt(_sumsq(cq) * (1.0 / Q_LORA) + EPS)
    cq_ref[...] = (cq * inv * gq_ref[...]).astype(cq_ref.dtype)
    ckv = acc[:, Q_LORA:Q_LORA + KV_LORA]
    inv = lax.rsqrt(_sumsq(ckv) * (1.0 / KV_LORA) + EPS)
    ckv_ref[...] = (ckv * inv * gkv_ref[...]).astype(ckv_ref.dtype)
    ang = pos_ref[...].astype(F32) * freq_ref[...]
    cos_t = jnp.cos(ang) * cmask_ref[...]
    sin_t = jnp.sin(ang) * smask_ref[...]
    base = Q_LORA + KV_LORA
    ka = acc[:, base:base + LANES]
    kb = acc[:, base + LANES:base + 2 * LANES]
    krot_ref[...] = ka * cos_t + kb * sin_t
    cos_ref[...] = cos_t
    sin_ref[...] = sin_t


def _ep_mla_q(scale, acc, extra, outs):
    cos_ref, sin_ref, gn_ref, gr_ref = extra
    (o_ref,) = outs
    nope = acc[:, :LANES]
    rot = acc[:, LANES:2 * LANES] * cos_ref[...] + acc[:, 2 * LANES:] * sin_ref[...]
    inv = lax.rsqrt((_sumsq(nope) + _sumsq(rot)) * (1.0 / QK_DIM) + EPS) * scale
    o_ref[:, :LANES] = (nope * inv * gn_ref[...]).astype(o_ref.dtype)
    o_ref[:, LANES:] = (rot * inv * gr_ref[...]).astype(o_ref.dtype)


def _ep_mla_kv(acc, extra, outs):
    krot_ref, gn_ref, gr_ref = extra
    k_ref, v_ref = outs
    nope = acc[:, :LANES]
    rot = krot_ref[...]
    inv = lax.rsqrt((_sumsq(nope) + _sumsq(rot)) * (1.0 / QK_DIM) + EPS)
    k_ref[:, :LANES] = (nope * inv * gn_ref[...]).astype(k_ref.dtype)
    k_ref[:, LANES:] = (rot * inv * gr_ref[...]).astype(k_ref.dtype)
    v_ref[...] = acc[:, LANES:].astype(v_ref.dtype)


def _ep_mem_attn(acc, extra, outs):
    mk_ref, mv_ref, gq_ref = extra
    (o_ref,) = outs
    scale = 1.0 / np.sqrt(MEM_HEAD_DIM)
    for h in range(MEM_HEADS):
        sl = slice(h * MEM_HEAD_DIM, (h + 1) * MEM_HEAD_DIM)
        qh = acc[:, sl]
        inv = lax.rsqrt(_sumsq(qh) * (1.0 / MEM_HEAD_DIM) + EPS) * scale
        qn = (qh * inv * gq_ref[...]).astype(BF16)
        s = lax.dot_general(qn, mk_ref[:, sl], (((1,), (1,)), ((), ())), preferred_element_type=F32)
        p = jnp.exp(s - jnp.max(s, axis=-1, keepdims=True))
        den = jnp.sum(p, axis=-1, keepdims=True)
        o = jnp.dot(p.astype(BF16), mv_ref[:, sl], preferred_element_type=F32)
        o_ref[:, sl] = (o / den).astype(o_ref.dtype)


def _ep_relu2(acc, extra, outs):
    (o_ref,) = outs
    u = jnp.maximum(acc, 0.0)
    o_ref[...] = (u * u).astype(o_ref.dtype)


def _ep_residual(acc, extra, outs):
    (r_ref,) = extra
    (o_ref,) = outs
    o_ref[...] = r_ref[...] + acc


def _memkv_kernel(mem_ref, g_ref, w_ref, gk_ref, o_ref):
    x = mem_ref[...]
    ms = jnp.mean(x * x, axis=-1, keepdims=True)
    hn = (x * lax.rsqrt(ms + EPS) * g_ref[...]).astype(BF16)
    acc = jnp.dot(hn, w_ref[...], preferred_element_type=F32)
    j = pl.program_id(0)
    inv = lax.rsqrt(_sumsq(acc) * (1.0 / MEM_HEAD_DIM) + EPS)
    normed = acc * inv * gk_ref[...]
    o_ref[...] = jnp.where(j < MEM_HEADS, normed, acc).astype(o_ref.dtype)


def _memkv(mem, g_mem, w_mem_kv_bf16, g_k_mem):
    n_mem, d = mem.shape
    n = w_mem_kv_bf16.shape[1]
    return pl.pallas_call(
        _memkv_kernel,
        grid=(n // MEM_HEAD_DIM,),
        in_specs=[
            pl.BlockSpec((n_mem, d), lambda j: (0, 0)),
            pl.BlockSpec((1, d), lambda j: (0, 0)),
            pl.BlockSpec((d, MEM_HEAD_DIM), lambda j: (0, j)),
            pl.BlockSpec((1, MEM_HEAD_DIM), lambda j: (0, 0)),
        ],
        out_specs=pl.BlockSpec((n_mem, MEM_HEAD_DIM), lambda j: (0, j)),
        out_shape=jax.ShapeDtypeStruct((n_mem, n), BF16),
        compiler_params=_params("parallel"),
        name="mem_kv",
    )(mem, g_mem.reshape(1, d), w_mem_kv_bf16, g_k_mem.reshape(1, MEM_HEAD_DIM))


def _sb_attn_kernel(q_ref, k_ref, v_ref, tri_ref, o_ref, acc_ref, carry_ref):
    t = q_ref.shape[0]
    i = pl.program_id(1)
    q = q_ref[...]
    row = lax.broadcasted_iota(jnp.int32, (t, t), 0)
    col = lax.broadcasted_iota(jnp.int32, (t, t), 1)
    causal = col < row

    def tile(j, masked):
        start = pl.multiple_of(j * t, t)
        k = k_ref[pl.ds(start, t), :]
        v = v_ref[pl.ds(start, t), :]
        nz = lax.dot_general(q, k, (((1,), (1,)), ((), ())), preferred_element_type=F32)
        l = jnp.minimum(nz, 0.0) - jnp.log(1.0 + jnp.exp(-jnp.abs(nz)))
        if masked:
            l = jnp.where(causal, l, 0.0)
        l_hi = l.astype(BF16)
        l_lo = (l - l_hi.astype(F32)).astype(BF16)
        suffix = jnp.dot(jnp.concatenate([l_hi, l_lo], axis=1), tri_ref[...], preferred_element_type=F32)
        carry = carry_ref[...]
        a = jnp.exp((l - nz) + suffix + carry)
        if masked:
            a = jnp.where(causal, a, 0.0)
        acc_ref[...] += jnp.dot(a.astype(BF16), v, preferred_element_type=F32)
        new_carry = carry + (suffix[:, 0:1] + l[:, 0:1])
        carry_ref[...] = new_carry
        return jnp.max(new_carry)

    acc_ref[...] = jnp.zeros_like(acc_ref)
    carry_ref[...] = jnp.zeros_like(carry_ref)
    top = tile(i, True)

    def cond(state):
        j, top = state
        return jnp.logical_and(j >= 0, top > SB_DEAD_LOG)

    def body(state):
        j, _ = state
        return j - 1, tile(j, False)

    lax.while_loop(cond, body, (i - 1, top))
    o_ref[...] = acc_ref[...].astype(o_ref.dtype)


def _sb_attention(qkv):
    s = qkv.shape[0]
    t = _tile(s, ATTN_TILE)
    tri = (np.arange(t)[:, None] > np.arange(t)[None, :]).astype(np.float32)
    tri = jnp.asarray(np.concatenate([tri, tri], axis=0), dtype=BF16)
    hd = SB_HEAD_DIM
    return pl.pallas_call(
        _sb_attn_kernel,
        grid=(SB_HEADS, s // t),
        in_specs=[
            pl.BlockSpec((t, hd), lambda h, i: (i, h)),
            pl.BlockSpec((s, hd), lambda h, i: (0, SB_HEADS + h)),
            pl.BlockSpec((s, hd), lambda h, i: (0, 2 * SB_HEADS + h)),
            pl.BlockSpec((2 * t, t), lambda h, i: (0, 0)),
        ],
        out_specs=pl.BlockSpec((t, hd), lambda h, i: (i, h)),
        out_shape=jax.ShapeDtypeStruct((s, SB_W), BF16),
        scratch_shapes=[pltpu.VMEM((t, hd), F32), pltpu.VMEM((t, 1), F32)],
        compiler_params=_params("parallel", "parallel"),
        name="sb_attention",
    )(qkv, qkv, qkv, tri)


def _mla_attn_kernel(q_ref, k_ref, v_ref, o_ref, acc_ref, m_ref, l_ref):
    t = q_ref.shape[0]
    i = pl.program_id(1)
    q = q_ref[...]
    row = lax.broadcasted_iota(jnp.int32, (t, t), 0)
    col = lax.broadcasted_iota(jnp.int32, (t, t), 1)
    visible = (col // CHUNK) <= (row // CHUNK)

    def tile(j, masked):
        start = pl.multiple_of(j * t, t)
        k = k_ref[pl.ds(start, t), :]
        v = v_ref[pl.ds(start, t), :]
        s = lax.dot_general(q, k, (((1,), (1,)), ((), ())), preferred_element_type=F32)
        if masked:
            s = jnp.where(visible, s, -jnp.inf)
        m_prev = m_ref[...]
        m_new = jnp.maximum(m_prev, jnp.max(s, axis=-1, keepdims=True))
        alpha = jnp.exp(m_prev - m_new)
        p = jnp.exp(s - m_new)
        l_ref[...] = alpha * l_ref[...] + jnp.sum(p, axis=-1, keepdims=True)
        acc_ref[...] = alpha * acc_ref[...] + jnp.dot(p.astype(BF16), v, preferred_element_type=F32)
        m_ref[...] = m_new

    acc_ref[...] = jnp.zeros_like(acc_ref)
    l_ref[...] = jnp.zeros_like(l_ref)
    m_ref[...] = jnp.full_like(m_ref, -jnp.inf)
    tile(i, True)

    def body(j, c):
        tile(j, False)
        return c

    lax.fori_loop(0, i, body, 0)
    o_ref[...] = (acc_ref[...] / l_ref[...]).astype(o_ref.dtype)


def _mla_attention(q, k, v):
    s = q.shape[0]
    t = _tile(s, ATTN_TILE)
    dk = 2 * LANES
    return pl.pallas_call(
        _mla_attn_kernel,
        grid=(MLA_HEADS, s // t),
        in_specs=[
            pl.BlockSpec((t, dk), lambda h, i: (i, h)),
            pl.BlockSpec((s, dk), lambda h, i: (0, h)),
            pl.BlockSpec((s, V_DIM), lambda h, i: (0, h)),
        ],
        out_specs=pl.BlockSpec((t, V_DIM), lambda h, i: (i, h)),
        out_shape=jax.ShapeDtypeStruct((s, MLA_HEADS * V_DIM), BF16),
        scratch_shapes=[pltpu.VMEM((t, V_DIM), F32), pltpu.VMEM((t, 1), F32), pltpu.VMEM((t, 1), F32)],
        compiler_params=_params("parallel", "parallel"),
        name="mla_attention",
    )(q, k, v)


def _merge_kernel(osb_ref, omla_ref, omem_ref, wsb_ref, wmla_ref, wmem_ref, g0_ref, g1_ref, g2_ref, o_ref):
    acc = g0_ref[...].astype(F32) * jnp.dot(osb_ref[...], wsb_ref[...], preferred_element_type=F32)
    acc += g1_ref[...].astype(F32) * jnp.dot(omla_ref[...], wmla_ref[...], preferred_element_type=F32)
    acc += g2_ref[...].astype(F32) * jnp.dot(omem_ref[...], wmem_ref[...], preferred_element_type=F32)
    o_ref[...] = acc.astype(o_ref.dtype)


def _merge(o_sb, o_mla, o_mem, w_sb_o, w_mla_o, w_mem_o, gates):
    m = o_sb.shape[0]
    d = w_sb_o.shape[1]
    tm = _tile(m, 1024)
    tn = _tile(d, 1024)
    nj = d // tn
    row = lambda i, j: (i, 0)
    colw = lambda i, j: (0, j)
    return pl.pallas_call(
        _merge_kernel,
        grid=(m // tm, nj),
        in_specs=[
            pl.BlockSpec((tm, o_sb.shape[1]), row),
            pl.BlockSpec((tm, o_mla.shape[1]), row),
            pl.BlockSpec((tm, o_mem.shape[1]), row),
            pl.BlockSpec((w_sb_o.shape[0], tn), colw),
            pl.BlockSpec((w_mla_o.shape[0], tn), colw),
            pl.BlockSpec((w_mem_o.shape[0], tn), colw),
            pl.BlockSpec((tm, tn), lambda i, j: (i, j)),
            pl.BlockSpec((tm, tn), lambda i, j: (i, nj + j)),
            pl.BlockSpec((tm, tn), lambda i, j: (i, 2 * nj + j)),
        ],
        out_specs=pl.BlockSpec((tm, tn), lambda i, j: (i, j)),
        out_shape=jax.ShapeDtypeStruct((m, d), BF16),
        compiler_params=_params("parallel", "parallel"),
        name="gated_merge",
    )(o_sb, o_mla, o_mem, w_sb_o, w_mla_o, w_mem_o, gates, gates, gates)


def _mm_kacc_kernel(a_ref, b_ref, r_ref, o_ref):
    @pl.when(pl.program_id(2) == 0)
    def _():
        o_ref[...] = r_ref[...]

    o_ref[...] += jnp.dot(a_ref[...], b_ref[...], preferred_element_type=F32)


def _matmul_kacc_residual(a, b, r, name):
    m, k = a.shape
    n = b.shape[1]
    tm, tn, tk = _tile(m, 1024), _tile(n, 1024), _tile(k, 2048)
    return pl.pallas_call(
        _mm_kacc_kernel,
        grid=(m // tm, n // tn, k // tk),
        in_specs=[
            pl.BlockSpec((tm, tk), lambda i, j, kk: (i, kk)),
            pl.BlockSpec((tk, tn), lambda i, j, kk: (kk, j)),
            pl.BlockSpec((tm, tn), lambda i, j, kk: (i, j)),
        ],
        out_specs=pl.BlockSpec((tm, tn), lambda i, j, kk: (i, j)),
        out_shape=jax.ShapeDtypeStruct((m, n), F32),
        compiler_params=_params("parallel", "parallel", "arbitrary"),
        name=name,
    )(a, b, r)


def _rope_tables():
    half = ROPE_DIM // 2
    freqs = 1.0 / (ROPE_THETA ** (jnp.arange(half, dtype=F32) / half))
    zeros = jnp.zeros((LANES - ROPE_DIM,), F32)
    freq = jnp.concatenate([freqs, freqs, zeros]).reshape(1, LANES)
    cmask = jnp.concatenate([jnp.ones((ROPE_DIM,), F32), zeros]).reshape(1, LANES)
    smask = jnp.concatenate([-jnp.ones((half,), F32), jnp.ones((half,), F32), zeros]).reshape(1, LANES)
    return freq, cmask, smask


def _rope_cols(w):
    half = ROPE_DIM // 2
    pad = jnp.zeros(w.shape[:-1] + (LANES - ROPE_DIM,), w.dtype)
    a = jnp.concatenate([w, pad], axis=-1)
    b = jnp.concatenate([w[..., half:], w[..., :half], pad], axis=-1)
    return a, b


def _pad_gain(g):
    return jnp.concatenate([g, jnp.zeros((LANES - g.shape[0],), g.dtype)]).reshape(1, LANES)


def _layer(x, mem, positions, g_mix, g_mem, w_in, g_cq, g_ckv, w_q_b, w_kv_b, g_q_mla, g_k_mla, w_mem_kv,
           g_q_mem, g_k_mem, w_sb_o, w_mla_o, w_mem_o, w_out, g_ffn, w_ff1, w_ff2):
    s, d = x.shape
    pos = positions.reshape(s, 1)

    c0 = 3 * SB_W
    c1 = c0 + Q_LORA
    c2 = c1 + KV_LORA
    c3 = c2 + ROPE_DIM
    c4 = c3 + MEM_W
    w_sbqkv = w_in[:, :c0].astype(BF16)
    kpe_a, kpe_b = _rope_cols(w_in[:, c2:c3])
    w_lat = jnp.concatenate([w_in[:, c0:c2], kpe_a, kpe_b], axis=1).astype(BF16)
    w_memq = w_in[:, c3:c4].astype(BF16)
    w_gate = w_in[:, c4:].astype(BF16)
    wq = w_q_b.reshape(Q_LORA, MLA_HEADS, QK_DIM)
    wq_a, wq_b = _rope_cols(wq[..., NOPE_DIM:])
    w_qb = jnp.concatenate([wq[..., :NOPE_DIM], wq_a, wq_b], axis=-1).reshape(Q_LORA, MLA_HEADS * 3 * LANES)
    w_qb = w_qb.astype(BF16)
    w_kvb = w_kv_b.astype(BF16)

    h = _rmsnorm(x, g_mix, "rmsnorm_mix")

    tm = _tile(s, 1024)

    sb_scale = -1.0 / np.sqrt(SB_HEAD_DIM).astype(np.float32)
    colscale = jnp.concatenate([jnp.full((SB_W,), sb_scale, F32), jnp.ones((2 * SB_W,), F32)]).reshape(1, c0)
    tn = _tile(c0, 768)
    (sb_qkv,) = _matmul(
        h, w_sbqkv, tm=tm, tn=tn,
        extra=[(colscale, (1, tn), lambda i, j: (0, j))],
        outs=[((s, c0), BF16, (tm, tn), lambda i, j: (i, j))],
        epilogue=_ep_colscale, name="proj_sb_qkv")
    o_sb = _sb_attention(sb_qkv)

    freq, cmask, smask = _rope_tables()
    tml = _tile(s, 512)
    nlat = w_lat.shape[1]
    full = lambda shape: (shape, lambda i, j: (0, 0))
    rows = lambda w: ((tml, w), lambda i, j: (i, 0))
    cq_n, ckv_n, krot, cos_t, sin_t = _matmul(
        h, w_lat, tm=tml, tn=nlat,
        extra=[(pos, *rows(1)), (freq, *full((1, LANES))), (cmask, *full((1, LANES))), (smask, *full((1, LANES))),
               (g_cq.reshape(1, Q_LORA), *full((1, Q_LORA))), (g_ckv.reshape(1, KV_LORA), *full((1, KV_LORA)))],
        outs=[((s, Q_LORA), BF16, *rows(Q_LORA)), ((s, KV_LORA), BF16, *rows(KV_LORA)),
              ((s, LANES), F32, *rows(LANES)), ((s, LANES), F32, *rows(LANES)), ((s, LANES), F32, *rows(LANES))],
        epilogue=_ep_latent, name="proj_mla_latent")

    gq_n, gq_r = g_q_mla[:NOPE_DIM].reshape(1, LANES), _pad_gain(g_q_mla[NOPE_DIM:])
    gk_n, gk_r = g_k_mla[:NOPE_DIM].reshape(1, LANES), _pad_gain(g_k_mla[NOPE_DIM:])
    rows = lambda w: ((tm, w), lambda i, j: (i, 0))
    mla_scale = 1.0 / np.sqrt(QK_DIM).astype(np.float32)
    (q_mla,) = _matmul(
        cq_n, w_qb, tm=tm, tn=3 * LANES,
        extra=[(cos_t, *rows(LANES)), (sin_t, *rows(LANES)), (gq_n, *full((1, LANES))), (gq_r, *full((1, LANES)))],
        outs=[((s, MLA_HEADS * 2 * LANES), BF16, (tm, 2 * LANES), lambda i, j: (i, j))],
        epilogue=functools.partial(_ep_mla_q, mla_scale), name="mla_q_up")
    k_mla, v_mla = _matmul(
        ckv_n, w_kvb, tm=tm, tn=2 * LANES,
        extra=[(krot, *rows(LANES)), (gk_n, *full((1, LANES))), (gk_r, *full((1, LANES)))],
        outs=[((s, MLA_HEADS * 2 * LANES), BF16, (tm, 2 * LANES), lambda i, j: (i, j)),
              ((s, MLA_HEADS * V_DIM), BF16, (tm, V_DIM), lambda i, j: (i, j))],
        epilogue=_ep_mla_kv, name="mla_kv_up")
    o_mla = _mla_attention(q_mla, k_mla, v_mla)

    mkv = _memkv(mem, g_mem, w_mem_kv.astype(BF16), g_k_mem)
    n_mem = mem.shape[0]
    tmm = _tile(s, 512)
    (o_mem,) = _matmul(
        h, w_memq, tm=tmm, tn=MEM_W,
        extra=[(mkv, (n_mem, MEM_W), lambda i, j: (0, 0)), (mkv, (n_mem, MEM_W), lambda i, j: (0, 1)),
               (g_q_mem.reshape(1, MEM_HEAD_DIM), (1, MEM_HEAD_DIM), lambda i, j: (0, 0))],
        outs=[((s, MEM_W), BF16, (tmm, MEM_W), lambda i, j: (i, 0))],
        epilogue=_ep_mem_attn, name="mem_attention")

    tng = _tile(3 * d, 1024)
    (gates,) = _matmul(
        h, w_gate, tm=tm, tn=tng,
        outs=[((s, 3 * d), BF16, (tm, tng), lambda i, j: (i, j))],
        epilogue=_ep_sigmoid, name="proj_gates")
    merged = _merge(o_sb, o_mla, o_mem, w_sb_o.astype(BF16), w_mla_o.astype(BF16), w_mem_o.astype(BF16), gates)
    tnd = _tile(d, 1024)
    (x1,) = _matmul(
        merged, w_out.astype(BF16), tm=tm, tn=tnd,
        extra=[(x, (tm, tnd), lambda i, j: (i, j))],
        outs=[((s, d), F32, (tm, tnd), lambda i, j: (i, j))],
        epilogue=_ep_residual, name="out_proj")

    hn = _rmsnorm(x1, g_ffn, "rmsnorm_ffn")
    dff = w_ff1.shape[1]
    tnf = _tile(dff, 1024)
    (u2,) = _matmul(
        hn, w_ff1.astype(BF16), tm=tm, tn=tnf,
        outs=[((s, dff), BF16, (tm, tnf), lambda i, j: (i, j))],
        epilogue=_ep_relu2, name="ffn_up")
    return _matmul_kacc_residual(u2, w_ff2.astype(BF16), x1, "ffn_down")


def kernel(x, mem, positions, g_mix, g_mem, w_in, g_cq, g_ckv, w_q_b, w_kv_b, g_q_mla, g_k_mla, w_mem_kv, g_q_mem,
           g_k_mem, w_sb_o, w_mla_o, w_mem_o, w_out, g_ffn, w_ff1, w_ff2):
    depth = w_in.shape[0]
    assert x.shape[0] == 1 and mem.shape[0] == 1
    y = x[0]
    for i in range(depth):
        y = _layer(y, mem[0], positions[0], g_mix[i], g_mem[i], w_in[i], g_cq[i], g_ckv[i], w_q_b[i], w_kv_b[i],
                   g_q_mla[i], g_k_mla[i], w_mem_kv[i], g_q_mem[i], g_k_mem[i], w_sb_o[i], w_mla_o[i], w_mem_o[i],
                   w_out[i], g_ffn[i], w_ff1[i], w_ff2[i])
    return y[None]
```

```python
import functools

import jax
import jax.numpy as jnp
import numpy as np
from jax import lax
from jax.experimental import pallas as pl
from jax.experimental.pallas import tpu as pltpu

F32 = jnp.float32
BF16 = jnp.bfloat16

CHUNK = 64
SB_HEADS = 12
SB_HEAD_DIM = 128
SB_W = SB_HEADS * SB_HEAD_DIM
MLA_HEADS = 12
Q_LORA = 896
KV_LORA = 512
NOPE_DIM = 128
ROPE_DIM = 64
V_DIM = 128
QK_DIM = NOPE_DIM + ROPE_DIM
MEM_HEADS = 4
MEM_HEAD_DIM = 256
MEM_W = MEM_HEADS * MEM_HEAD_DIM
ROPE_THETA = 10000.0
EPS = 1e-6

LANES = 128
VMEM_LIMIT_BYTES = 56 * 1024 * 1024
SB_DEAD_LOG = -120.0
ATTN_TILE = 256
MLA_TILE = 512


def _params(*sem):
    return pltpu.CompilerParams(dimension_semantics=sem, vmem_limit_bytes=VMEM_LIMIT_BYTES)


def _tile(n, pref):
    t = int(min(n, pref))
    assert n % t == 0, (n, t)
    return t


def _rmsnorm_kernel(x_ref, g_ref, o_ref):
    x = x_ref[...]
    ms = jnp.mean(x * x, axis=-1, keepdims=True)
    o_ref[...] = (x * lax.rsqrt(ms + EPS) * g_ref[...]).astype(o_ref.dtype)


def _rmsnorm(x, g, name):
    m, d = x.shape
    tm = _tile(m, 256)
    return pl.pallas_call(
        _rmsnorm_kernel,
        grid=(m // tm,),
        in_specs=[pl.BlockSpec((tm, d), lambda i: (i, 0)), pl.BlockSpec((1, d), lambda i: (0, 0))],
        out_specs=pl.BlockSpec((tm, d), lambda i: (i, 0)),
        out_shape=jax.ShapeDtypeStruct((m, d), BF16),
        compiler_params=_params("parallel"),
        name=name,
    )(x, g.reshape(1, d))


def _cast_pad_kernel(cut, w_ref, o_ref):
    pad = LANES - ROPE_DIM
    o_ref[:, :cut] = w_ref[:, :cut].astype(o_ref.dtype)
    o_ref[:, cut:cut + pad] = jnp.zeros((o_ref.shape[0], pad), o_ref.dtype)
    o_ref[:, cut + pad:] = w_ref[:, cut:].astype(o_ref.dtype)


def _cast_pad_w_in(w_in, cut):
    k, n = w_in.shape
    tr = _tile(k, 128)
    n_out = n + LANES - ROPE_DIM
    return pl.pallas_call(
        functools.partial(_cast_pad_kernel, cut),
        grid=(k // tr,),
        in_specs=[pl.BlockSpec((tr, n), lambda i: (i, 0))],
        out_specs=pl.BlockSpec((tr, n_out), lambda i: (i, 0)),
        out_shape=jax.ShapeDtypeStruct((k, n_out), BF16),
        compiler_params=_params("parallel"),
        name="cast_w_in",
    )(w_in)


def _mm_kernel(epilogue, n_extra, a_ref, b_ref, *refs):
    acc = jnp.dot(a_ref[...], b_ref[...], preferred_element_type=F32)
    epilogue(acc, refs[:n_extra], refs[n_extra:])


def _matmul(a, b, *, tm, tn, extra=(), outs, epilogue, name, b_col0=0, n=None):
    m, k = a.shape
    k2 = b.shape[0]
    n = b.shape[1] if n is None else n
    assert k == k2 and m % tm == 0 and n % tn == 0 and b_col0 % tn == 0 and b_col0 + n <= b.shape[1]
    jb = b_col0 // tn
    in_specs = [pl.BlockSpec((tm, k), lambda i, j: (i, 0)), pl.BlockSpec((k, tn), lambda i, j: (0, jb + j))]
    in_specs += [pl.BlockSpec(bs, im) for _, bs, im in extra]
    out_specs = [pl.BlockSpec(bs, im) for _, _, bs, im in outs]
    out_shape = [jax.ShapeDtypeStruct(s, dt) for s, dt, _, _ in outs]
    res = pl.pallas_call(
        functools.partial(_mm_kernel, epilogue, len(extra)),
        grid=(m // tm, n // tn),
        in_specs=in_specs,
        out_specs=out_specs,
        out_shape=out_shape,
        compiler_params=_params("parallel", "parallel"),
        name=name,
    )(a, b, *[x for x, _, _ in extra])
    return res


def _sumsq(x):
    return jnp.sum(x * x, axis=-1, keepdims=True)


def _ep_colscale(acc, extra, outs):
    (scale_ref,) = extra
    (o_ref,) = outs
    o_ref[...] = (acc * scale_ref[...]).astype(o_ref.dtype)


def _ep_sigmoid(acc, extra, outs):
    (o_ref,) = outs
    o_ref[...] = (1.0 / (1.0 + jnp.exp(-acc))).astype(o_ref.dtype)


def _ep_latent(acc, extra, outs):
    pos_ref, freq_ref, cmask_ref, smask_ref, gq_ref, gkv_ref = extra
    cq_ref, ckv_ref, krot_ref, cos_ref, sin_ref = outs
    cq = acc[:, :Q_LORA]
    inv = lax.rsqrt(_sumsq(cq) * (1.0 / Q_LORA) + EPS)
    cq_ref[...] = (cq * inv * gq_ref[...]).astype(cq_ref.dtype)
    ckv = acc[:, Q_LORA:Q_LORA + KV_LORA]
    inv = lax.rsqrt(_sumsq(ckv) * (1.0 / KV_LORA) + EPS)
    ckv_ref[...] = (ckv * inv * gkv_ref[...]).astype(ckv_ref.dtype)
    ang = pos_ref[...].astype(F32) * freq_ref[...]
    cos_t = jnp.cos(ang) * cmask_ref[...]
    sin_t = jnp.sin(ang) * smask_ref[...]
    base = Q_LORA + KV_LORA
    ka = acc[:, base:base + LANES]
    kb = acc[:, base + LANES:base + 2 * LANES]
    krot_ref[...] = ka * cos_t + kb * sin_t
    cos_ref[...] = cos_t
    sin_ref[...] = sin_t


def _ep_mla_q(scale, acc, extra, outs):
    cos_ref, sin_ref, gn_ref, gr_ref = extra
    (o_ref,) = outs
    nope = acc[:, :LANES]
    rot = acc[:, LANES:2 * LANES] * cos_ref[...] + acc[:, 2 * LANES:] * sin_ref[...]
    inv = lax.rsqrt((_sumsq(nope) + _sumsq(rot)) * (1.0 / QK_DIM) + EPS) * scale
    o_ref[:, :LANES] = (nope * inv * gn_ref[...]).astype(o_ref.dtype)
    o_ref[:, LANES:] = (rot * inv * gr_ref[...]).astype(o_ref.dtype)


def _ep_mla_kv(acc, extra, outs):
    krot_ref, gn_ref, gr_ref = extra
    k_ref, v_ref = outs
    nope = acc[:, :LANES]
    rot = krot_ref[...]
    inv = lax.rsqrt((_sumsq(nope) + _sumsq(rot)) * (1.0 / QK_DIM) + EPS)
    k_ref[:, :LANES] = (nope * inv * gn_ref[...]).astype(k_ref.dtype)
    k_ref[:, LANES:] = (rot * inv * gr_ref[...]).astype(k_ref.dtype)
    v_ref[...] = acc[:, LANES:].astype(v_ref.dtype)


def _ep_mem_attn(acc, extra, outs):
    mk_ref, mv_ref, gq_ref = extra
    (o_ref,) = outs
    scale = 1.0 / np.sqrt(MEM_HEAD_DIM)
    for h in range(MEM_HEADS):
        sl = slice(h * MEM_HEAD_DIM, (h + 1) * MEM_HEAD_DIM)
        qh = acc[:, sl]
        inv = lax.rsqrt(_sumsq(qh) * (1.0 / MEM_HEAD_DIM) + EPS) * scale
        qn = (qh * inv * gq_ref[...]).astype(BF16)
        s = lax.dot_general(qn, mk_ref[:, sl], (((1,), (1,)), ((), ())), preferred_element_type=F32)
        p = jnp.exp(s - jnp.max(s, axis=-1, keepdims=True))
        den = jnp.sum(p, axis=-1, keepdims=True)
        o = jnp.dot(p.astype(BF16), mv_ref[:, sl], preferred_element_type=F32)
        o_ref[:, sl] = (o / den).astype(o_ref.dtype)


def _ep_relu2(acc, extra, outs):
    (o_ref,) = outs
    u = jnp.maximum(acc, 0.0)
    o_ref[...] = (u * u).astype(o_ref.dtype)


def _ep_residual(acc, extra, outs):
    (r_ref,) = extra
    (o_ref,) = outs
    o_ref[...] = r_ref[...] + acc


def _memkv_kernel(mem_ref, g_ref, w_ref, gk_ref, o_ref):
    x = mem_ref[...]
    ms = jnp.mean(x * x, axis=-1, keepdims=True)
    hn = (x * lax.rsqrt(ms + EPS) * g_ref[...]).astype(BF16)
    acc = jnp.dot(hn, w_ref[...], preferred_element_type=F32)
    j = pl.program_id(0)
    inv = lax.rsqrt(_sumsq(acc) * (1.0 / MEM_HEAD_DIM) + EPS)
    normed = acc * inv * gk_ref[...]
    o_ref[...] = jnp.where(j < MEM_HEADS, normed, acc).astype(o_ref.dtype)


def _memkv(mem, g_mem, w_mem_kv_bf16, g_k_mem):
    n_mem, d = mem.shape
    n = w_mem_kv_bf16.shape[1]
    return pl.pallas_call(
        _memkv_kernel,
        grid=(n // MEM_HEAD_DIM,),
        in_specs=[
            pl.BlockSpec((n_mem, d), lambda j: (0, 0)),
            pl.BlockSpec((1, d), lambda j: (0, 0)),
            pl.BlockSpec((d, MEM_HEAD_DIM), lambda j: (0, j)),
            pl.BlockSpec((1, MEM_HEAD_DIM), lambda j: (0, 0)),
        ],
        out_specs=pl.BlockSpec((n_mem, MEM_HEAD_DIM), lambda j: (0, j)),
        out_shape=jax.ShapeDtypeStruct((n_mem, n), BF16),
        compiler_params=_params("parallel"),
        name="mem_kv",
    )(mem, g_mem.reshape(1, d), w_mem_kv_bf16, g_k_mem.reshape(1, MEM_HEAD_DIM))


def _sb_attn_kernel(q_ref, k_ref, v_ref, tri_win_ref, tri_ref, o_ref, acc_ref, carry_ref):
    t = q_ref.shape[0]
    i = pl.program_id(1)
    q = q_ref[...]

    def block(start, width, tri_ref, mask_offset):
        k = k_ref[pl.ds(start, width), :]
        v = v_ref[pl.ds(start, width), :]
        nz = lax.dot_general(q, k, (((1,), (1,)), ((), ())), preferred_element_type=F32)
        l = jnp.minimum(nz, 0.0) - jnp.log(1.0 + jnp.exp(-jnp.abs(nz)))
        if mask_offset is not None:
            row = lax.broadcasted_iota(jnp.int32, (t, width), 0)
            col = lax.broadcasted_iota(jnp.int32, (t, width), 1)
            causal = col < row + mask_offset
            l = jnp.where(causal, l, 0.0)
        l_hi = l.astype(BF16)
        l_lo = (l - l_hi.astype(F32)).astype(BF16)
        suffix = jnp.dot(jnp.concatenate([l_hi, l_lo], axis=1), tri_ref[...], preferred_element_type=F32)
        carry = carry_ref[...]
        a = jnp.exp((l - nz) + suffix + jnp.concatenate([carry] * (width // LANES), axis=1))
        if mask_offset is not None:
            a = jnp.where(causal, a, 0.0)
        acc_ref[...] += jnp.dot(a.astype(BF16), v, preferred_element_type=F32)
        new_carry = carry + jnp.broadcast_to(suffix[:, 0:1] + l[:, 0:1], carry.shape)
        carry_ref[...] = new_carry
        return jnp.max(new_carry)

    acc_ref[...] = jnp.zeros_like(acc_ref)
    carry_ref[...] = jnp.zeros_like(carry_ref)
    first = jnp.maximum(i - 1, 0)
    top = block(pl.multiple_of(first * t, t), 2 * t, tri_win_ref, (i - first) * t)

    def cond(state):
        j, top = state
        return jnp.logical_and(j >= 0, top > SB_DEAD_LOG)

    def body(state):
        j, _ = state
        return j - 1, block(pl.multiple_of(j * t, t), t, tri_ref, None)

    lax.while_loop(cond, body, (i - 2, top))
    o_ref[...] = acc_ref[...].astype(o_ref.dtype)


def _suffix_ones(width):
    tri = (np.arange(width)[:, None] > np.arange(width)[None, :]).astype(np.float32)
    return jnp.asarray(np.concatenate([tri, tri], axis=0), dtype=BF16)


def _sb_attention(qkv):
    s = qkv.shape[0]
    t = _tile(s, ATTN_TILE)
    assert s >= 2 * t
    hd = SB_HEAD_DIM
    return pl.pallas_call(
        _sb_attn_kernel,
        grid=(SB_HEADS, s // t),
        in_specs=[
            pl.BlockSpec((t, hd), lambda h, i: (i, h)),
            pl.BlockSpec((s, hd), lambda h, i: (0, SB_HEADS + h)),
            pl.BlockSpec((s, hd), lambda h, i: (0, 2 * SB_HEADS + h)),
            pl.BlockSpec((4 * t, 2 * t), lambda h, i: (0, 0)),
            pl.BlockSpec((2 * t, t), lambda h, i: (0, 0)),
        ],
        out_specs=pl.BlockSpec((t, hd), lambda h, i: (i, h)),
        out_shape=jax.ShapeDtypeStruct((s, SB_W), BF16),
        scratch_shapes=[pltpu.VMEM((t, hd), F32), pltpu.VMEM((t, LANES), F32)],
        compiler_params=_params("parallel", "parallel"),
        name="sb_attention",
    )(qkv, qkv, qkv, _suffix_ones(2 * t), _suffix_ones(t))


def _mla_attn_kernel(q_ref, k_ref, v_ref, o_ref, acc_ref, m_ref, l_ref):
    t = q_ref.shape[0]
    i = pl.program_id(1)
    q = q_ref[...]
    reps = t // LANES

    def tile(j, masked):
        start = pl.multiple_of(j * t, t)
        k = k_ref[pl.ds(start, t), :]
        v = v_ref[pl.ds(start, t), :]
        s = lax.dot_general(q, k, (((1,), (1,)), ((), ())), preferred_element_type=F32)
        if masked:
            row = lax.broadcasted_iota(jnp.int32, (t, t), 0)
            col = lax.broadcasted_iota(jnp.int32, (t, t), 1)
            s = jnp.where((col // CHUNK) <= (row // CHUNK), s, -jnp.inf)
        m_prev = m_ref[...]
        m_new = jnp.maximum(m_prev, jnp.max(s, axis=-1, keepdims=True))
        alpha = jnp.exp2(m_prev - m_new)
        p = jnp.exp2(s - jnp.concatenate([m_new] * reps, axis=1))
        l_ref[...] = alpha * l_ref[...] + jnp.sum(p, axis=-1, keepdims=True)
        acc_ref[...] = alpha * acc_ref[...] + jnp.dot(p.astype(BF16), v, preferred_element_type=F32)
        m_ref[...] = m_new

    acc_ref[...] = jnp.zeros_like(acc_ref)
    l_ref[...] = jnp.zeros_like(l_ref)
    m_ref[...] = jnp.full_like(m_ref, -jnp.inf)
    tile(i, True)

    def pair(p, c):
        tile(2 * p, False)
        tile(2 * p + 1, False)
        return c

    lax.fori_loop(0, i // 2, pair, 0)

    @pl.when(i % 2 == 1)
    def _():
        tile(i - 1, False)

    o_ref[...] = (acc_ref[...] / l_ref[...]).astype(o_ref.dtype)


def _mla_attention(q, k, v):
    s = q.shape[0]
    t = _tile(s, MLA_TILE)
    dk = 2 * LANES
    return pl.pallas_call(
        _mla_attn_kernel,
        grid=(MLA_HEADS, s // t),
        in_specs=[
            pl.BlockSpec((t, dk), lambda h, i: (i, h)),
            pl.BlockSpec((s, dk), lambda h, i: (0, h)),
            pl.BlockSpec((s, V_DIM), lambda h, i: (0, h)),
        ],
        out_specs=pl.BlockSpec((t, V_DIM), lambda h, i: (i, h)),
        out_shape=jax.ShapeDtypeStruct((s, MLA_HEADS * V_DIM), BF16),
        scratch_shapes=[pltpu.VMEM((t, V_DIM), F32), pltpu.VMEM((t, LANES), F32), pltpu.VMEM((t, LANES), F32)],
        compiler_params=_params("parallel", "parallel"),
        name="mla_attention",
    )(q, k, v)


def _merge_kernel(osb_ref, omla_ref, omem_ref, wsb_ref, wmla_ref, wmem_ref, g0_ref, g1_ref, g2_ref, o_ref):
    acc = g0_ref[...].astype(F32) * jnp.dot(osb_ref[...], wsb_ref[...], preferred_element_type=F32)
    acc += g1_ref[...].astype(F32) * jnp.dot(omla_ref[...], wmla_ref[...], preferred_element_type=F32)
    acc += g2_ref[...].astype(F32) * jnp.dot(omem_ref[...], wmem_ref[...], preferred_element_type=F32)
    o_ref[...] = acc.astype(o_ref.dtype)


def _merge(o_sb, o_mla, o_mem, w_sb_o, w_mla_o, w_mem_o, gates):
    m = o_sb.shape[0]
    d = w_sb_o.shape[1]
    tm = _tile(m, 1024)
    tn = _tile(d, 1024)
    nj = d // tn
    row = lambda i, j: (i, 0)
    colw = lambda i, j: (0, j)
    return pl.pallas_call(
        _merge_kernel,
        grid=(m // tm, nj),
        in_specs=[
            pl.BlockSpec((tm, o_sb.shape[1]), row),
            pl.BlockSpec((tm, o_mla.shape[1]), row),
            pl.BlockSpec((tm, o_mem.shape[1]), row),
            pl.BlockSpec((w_sb_o.shape[0], tn), colw),
            pl.BlockSpec((w_mla_o.shape[0], tn), colw),
            pl.BlockSpec((w_mem_o.shape[0], tn), colw),
            pl.BlockSpec((tm, tn), lambda i, j: (i, j)),
            pl.BlockSpec((tm, tn), lambda i, j: (i, nj + j)),
            pl.BlockSpec((tm, tn), lambda i, j: (i, 2 * nj + j)),
        ],
        out_specs=pl.BlockSpec((tm, tn), lambda i, j: (i, j)),
        out_shape=jax.ShapeDtypeStruct((m, d), BF16),
        compiler_params=_params("parallel", "parallel"),
        name="gated_merge",
    )(o_sb, o_mla, o_mem, w_sb_o, w_mla_o, w_mem_o, gates, gates, gates)


def _mm_kacc_kernel(a_ref, b_ref, r_ref, o_ref):
    @pl.when(pl.program_id(2) == 0)
    def _():
        o_ref[...] = r_ref[...]

    o_ref[...] += jnp.dot(a_ref[...], b_ref[...], preferred_element_type=F32)


def _matmul_kacc_residual(a, b, r, name):
    m, k = a.shape
    n = b.shape[1]
    tm, tn, tk = _tile(m, 1024), _tile(n, 1024), _tile(k, 2048)
    return pl.pallas_call(
        _mm_kacc_kernel,
        grid=(m // tm, n // tn, k // tk),
        in_specs=[
            pl.BlockSpec((tm, tk), lambda i, j, kk: (i, kk)),
            pl.BlockSpec((tk, tn), lambda i, j, kk: (kk, j)),
            pl.BlockSpec((tm, tn), lambda i, j, kk: (i, j)),
        ],
        out_specs=pl.BlockSpec((tm, tn), lambda i, j, kk: (i, j)),
        out_shape=jax.ShapeDtypeStruct((m, n), F32),
        compiler_params=_params("parallel", "parallel", "arbitrary"),
        name=name,
    )(a, b, r)


def _rope_tables():
    half = ROPE_DIM // 2
    freqs = 1.0 / (ROPE_THETA ** (jnp.arange(half, dtype=F32) / half))
    zeros = jnp.zeros((LANES - ROPE_DIM,), F32)
    freq = jnp.concatenate([freqs, freqs, zeros]).reshape(1, LANES)
    cmask = jnp.concatenate([jnp.ones((ROPE_DIM,), F32), zeros]).reshape(1, LANES)
    smask = jnp.concatenate([-jnp.ones((half,), F32), jnp.ones((half,), F32), zeros]).reshape(1, LANES)
    return freq, cmask, smask


def _rope_cols(w):
    half = ROPE_DIM // 2
    pad = jnp.zeros(w.shape[:-1] + (LANES - ROPE_DIM,), w.dtype)
    a = jnp.concatenate([w, pad], axis=-1)
    b = jnp.concatenate([w[..., half:], w[..., :half], pad], axis=-1)
    return a, b


def _pad_gain(g):
    return jnp.concatenate([g, jnp.zeros((LANES - g.shape[0],), g.dtype)]).reshape(1, LANES)


def _layer(x, mem, positions, g_mix, g_mem, w_in, g_cq, g_ckv, w_q_b, w_kv_b, g_q_mla, g_k_mla, w_mem_kv,
           g_q_mem, g_k_mem, w_sb_o, w_mla_o, w_mem_o, w_out, g_ffn, w_ff1, w_ff2):
    s, d = x.shape
    pos = positions.reshape(s, 1)

    c0 = 3 * SB_W
    c1 = c0 + Q_LORA
    c2 = c1 + KV_LORA
    c3 = c2 + ROPE_DIM
    c4 = c3 + MEM_W
    w_pad = _cast_pad_w_in(w_in, c3)
    c3p = c3 + LANES - ROPE_DIM
    c4p = c3p + MEM_W
    kpe_a, kpe_b = _rope_cols(w_in[:, c2:c3])
    w_lat = jnp.concatenate([w_in[:, c0:c2], kpe_a, kpe_b], axis=1).astype(BF16)
    wq = w_q_b.reshape(Q_LORA, MLA_HEADS, QK_DIM)
    wq_a, wq_b = _rope_cols(wq[..., NOPE_DIM:])
    w_qb = jnp.concatenate([wq[..., :NOPE_DIM], wq_a, wq_b], axis=-1).reshape(Q_LORA, MLA_HEADS * 3 * LANES)
    w_qb = w_qb.astype(BF16)
    w_kvb = w_kv_b.astype(BF16)

    h = _rmsnorm(x, g_mix, "rmsnorm_mix")

    tm = _tile(s, 1024)

    sb_scale = -1.0 / np.sqrt(SB_HEAD_DIM).astype(np.float32)
    colscale = jnp.concatenate([jnp.full((SB_W,), sb_scale, F32), jnp.ones((2 * SB_W,), F32)]).reshape(1, c0)
    tn = _tile(c0, 768)
    (sb_qkv,) = _matmul(
        h, w_pad, n=c0, tm=tm, tn=tn,
        extra=[(colscale, (1, tn), lambda i, j: (0, j))],
        outs=[((s, c0), BF16, (tm, tn), lambda i, j: (i, j))],
        epilogue=_ep_colscale, name="proj_sb_qkv")
    o_sb = _sb_attention(sb_qkv)

    freq, cmask, smask = _rope_tables()
    tml = _tile(s, 512)
    nlat = w_lat.shape[1]
    full = lambda shape: (shape, lambda i, j: (0, 0))
    rows = lambda w: ((tml, w), lambda i, j: (i, 0))
    cq_n, ckv_n, krot, cos_t, sin_t = _matmul(
        h, w_lat, tm=tml, tn=nlat,
        extra=[(pos, *rows(1)), (freq, *full((1, LANES))), (cmask, *full((1, LANES))), (smask, *full((1, LANES))),
               (g_cq.reshape(1, Q_LORA), *full((1, Q_LORA))), (g_ckv.reshape(1, KV_LORA), *full((1, KV_LORA)))],
        outs=[((s, Q_LORA), BF16, *rows(Q_LORA)), ((s, KV_LORA), BF16, *rows(KV_LORA)),
              ((s, LANES), F32, *rows(LANES)), ((s, LANES), F32, *rows(LANES)), ((s, LANES), F32, *rows(LANES))],
        epilogue=_ep_latent, name="proj_mla_latent")

    gq_n, gq_r = g_q_mla[:NOPE_DIM].reshape(1, LANES), _pad_gain(g_q_mla[NOPE_DIM:])
    gk_n, gk_r = g_k_mla[:NOPE_DIM].reshape(1, LANES), _pad_gain(g_k_mla[NOPE_DIM:])
    rows = lambda w: ((tm, w), lambda i, j: (i, 0))
    mla_scale = np.float32(np.log2(np.e) / np.sqrt(QK_DIM))
    (q_mla,) = _matmul(
        cq_n, w_qb, tm=tm, tn=3 * LANES,
        extra=[(cos_t, *rows(LANES)), (sin_t, *rows(LANES)), (gq_n, *full((1, LANES))), (gq_r, *full((1, LANES)))],
        outs=[((s, MLA_HEADS * 2 * LANES), BF16, (tm, 2 * LANES), lambda i, j: (i, j))],
        epilogue=functools.partial(_ep_mla_q, mla_scale), name="mla_q_up")
    k_mla, v_mla = _matmul(
        ckv_n, w_kvb, tm=tm, tn=2 * LANES,
        extra=[(krot, *rows(LANES)), (gk_n, *full((1, LANES))), (gk_r, *full((1, LANES)))],
        outs=[((s, MLA_HEADS * 2 * LANES), BF16, (tm, 2 * LANES), lambda i, j: (i, j)),
              ((s, MLA_HEADS * V_DIM), BF16, (tm, V_DIM), lambda i, j: (i, j))],
        epilogue=_ep_mla_kv, name="mla_kv_up")
    o_mla = _mla_attention(q_mla, k_mla, v_mla)

    mkv = _memkv(mem, g_mem, w_mem_kv.astype(BF16), g_k_mem)
    n_mem = mem.shape[0]
    tmm = _tile(s, 512)
    (o_mem,) = _matmul(
        h, w_pad, b_col0=c3p, n=MEM_W, tm=tmm, tn=MEM_W,
        extra=[(mkv, (n_mem, MEM_W), lambda i, j: (0, 0)), (mkv, (n_mem, MEM_W), lambda i, j: (0, 1)),
               (g_q_mem.reshape(1, MEM_HEAD_DIM), (1, MEM_HEAD_DIM), lambda i, j: (0, 0))],
        outs=[((s, MEM_W), BF16, (tmm, MEM_W), lambda i, j: (i, 0))],
        epilogue=_ep_mem_attn, name="mem_attention")

    tng = _tile(np.gcd(3 * d, c4p), 1024)
    (gates,) = _matmul(
        h, w_pad, b_col0=c4p, n=3 * d, tm=tm, tn=tng,
        outs=[((s, 3 * d), BF16, (tm, tng), lambda i, j: (i, j))],
        epilogue=_ep_sigmoid, name="proj_gates")
    merged = _merge(o_sb, o_mla, o_mem, w_sb_o.astype(BF16), w_mla_o.astype(BF16), w_mem_o.astype(BF16), gates)
    tnd = _tile(d, 1024)
    (x1,) = _matmul(
        merged, w_out.astype(BF16), tm=tm, tn=tnd,
        extra=[(x, (tm, tnd), lambda i, j: (i, j))],
        outs=[((s, d), F32, (tm, tnd), lambda i, j: (i, j))],
        epilogue=_ep_residual, name="out_proj")

    hn = _rmsnorm(x1, g_ffn, "rmsnorm_ffn")
    dff = w_ff1.shape[1]
    tnf = _tile(dff, 1024)
    (u2,) = _matmul(
        hn, w_ff1.astype(BF16), tm=tm, tn=tnf,
        outs=[((s, dff), BF16, (tm, tnf), lambda i, j: (i, j))],
        epilogue=_ep_relu2, name="ffn_up")
    return _matmul_kacc_residual(u2, w_ff2.astype(BF16), x1, "ffn_down")


def kernel(x, mem, positions, g_mix, g_mem, w_in, g_cq, g_ckv, w_q_b, w_kv_b, g_q_mla, g_k_mla, w_mem_kv, g_q_mem,
           g_k_mem, w_sb_o, w_mla_o, w_mem_o, w_out, g_ffn, w_ff1, w_ff2):
    depth = w_in.shape[0]
    assert x.shape[0] == 1 and mem.shape[0] == 1
    y = x[0]
    for i in range(depth):
        y = _layer(y, mem[0], positions[0], g_mix[i], g_mem[i], w_in[i], g_cq[i], g_ckv[i], w_q_b[i], w_kv_b[i],
                   g_q_mla[i], g_k_mla[i], w_mem_kv[i], g_q_mem[i], g_k_mem[i], w_sb_o[i], w_mla_o[i], w_mem_o[i],
                   w_out[i], g_ffn[i], w_ff1[i], w_ff2[i])
    return y[None]
```

```python
import functools

import jax
import jax.numpy as jnp
import numpy as np
from jax import lax
from jax.experimental import pallas as pl
from jax.experimental.pallas import tpu as pltpu

F32 = jnp.float32
BF16 = jnp.bfloat16

CHUNK = 64
SB_HEADS = 12
SB_HEAD_DIM = 128
SB_W = SB_HEADS * SB_HEAD_DIM
MLA_HEADS = 12
Q_LORA = 896
KV_LORA = 512
NOPE_DIM = 128
ROPE_DIM = 64
V_DIM = 128
QK_DIM = NOPE_DIM + ROPE_DIM
MEM_HEADS = 4
MEM_HEAD_DIM = 256
MEM_W = MEM_HEADS * MEM_HEAD_DIM
ROPE_THETA = 10000.0
EPS = 1e-6

LANES = 128
BF16_SUBLANES = 16
VMEM_LIMIT_BYTES = 56 * 1024 * 1024
SB_DEAD_LOG = -120.0
ATTN_TILE = 256
MLA_TILE = 512
MLA_SHIFT_BOUND = 50.0


def _params(*sem):
    return pltpu.CompilerParams(dimension_semantics=sem, vmem_limit_bytes=VMEM_LIMIT_BYTES)


def _tile(n, pref):
    t = int(min(n, pref))
    assert n % t == 0, (n, t)
    return t


def _rmsnorm_kernel(x_ref, g_ref, o_ref):
    x = x_ref[...]
    ms = jnp.mean(x * x, axis=-1, keepdims=True)
    o_ref[...] = (x * lax.rsqrt(ms + EPS) * g_ref[...]).astype(o_ref.dtype)


def _rmsnorm(x, g, name):
    m, d = x.shape
    tm = _tile(m, 256)
    return pl.pallas_call(
        _rmsnorm_kernel,
        grid=(m // tm,),
        in_specs=[pl.BlockSpec((tm, d), lambda i: (i, 0)), pl.BlockSpec((1, d), lambda i: (0, 0))],
        out_specs=pl.BlockSpec((tm, d), lambda i: (i, 0)),
        out_shape=jax.ShapeDtypeStruct((m, d), BF16),
        compiler_params=_params("parallel"),
        name=name,
    )(x, g.reshape(1, d))


def _cast_pad_kernel(gaps, w_ref, o_ref):
    src = dst = 0
    for cut, pad in gaps:
        o_ref[:, dst:dst + cut - src] = w_ref[:, src:cut].astype(o_ref.dtype)
        dst += cut - src
        o_ref[:, dst:dst + pad] = jnp.zeros((o_ref.shape[0], pad), o_ref.dtype)
        dst += pad
        src = cut
    o_ref[:, dst:] = w_ref[:, src:].astype(o_ref.dtype)


def _cast_pad_w_in(w_in, gaps):
    k, n = w_in.shape
    tr = _tile(k, 128)
    n_out = n + sum(pad for _, pad in gaps)
    return pl.pallas_call(
        functools.partial(_cast_pad_kernel, tuple(gaps)),
        grid=(k // tr,),
        in_specs=[pl.BlockSpec((tr, n), lambda i: (i, 0))],
        out_specs=pl.BlockSpec((tr, n_out), lambda i: (i, 0)),
        out_shape=jax.ShapeDtypeStruct((k, n_out), BF16),
        compiler_params=_params("parallel"),
        name="cast_w_in",
    )(w_in)


def _mm_kernel(epilogue, n_extra, n_cast, a_ref, b_ref, *refs):
    n_out = len(refs) - n_extra - 2 * n_cast
    extra = refs[:n_extra]
    cast_in = refs[n_extra:n_extra + n_cast]
    outs = refs[n_extra + n_cast:n_extra + n_cast + n_out]
    cast_out = refs[n_extra + n_cast + n_out:]
    acc = jnp.dot(a_ref[...], b_ref[...], preferred_element_type=F32)
    epilogue(acc, extra, outs)
    for src_ref, dst_ref in zip(cast_in, cast_out):
        dst_ref[...] = src_ref[...].astype(dst_ref.dtype)


def _matmul(a, b, *, tm, tn, extra=(), outs, epilogue, name, b_col0=0, n=None, cast=()):
    m, k = a.shape
    k2 = b.shape[0]
    n = b.shape[1] if n is None else n
    assert k == k2 and m % tm == 0 and n % tn == 0 and b_col0 % tn == 0 and b_col0 + n <= b.shape[1]
    jb = b_col0 // tn
    gm, gn = m // tm, n // tn
    in_specs = [pl.BlockSpec((tm, k), lambda i, j: (i, 0)), pl.BlockSpec((k, tn), lambda i, j: (0, jb + j))]
    in_specs += [pl.BlockSpec(bs, im) for _, bs, im in extra]
    out_specs = [pl.BlockSpec(bs, im) for _, _, bs, im in outs]
    out_shape = [jax.ShapeDtypeStruct(s, dt) for s, dt, _, _ in outs]
    for w in cast:
        rows, cols = w.shape
        slab = rows // (gm * gn)
        assert slab * gm * gn == rows and slab % BF16_SUBLANES == 0, (name, w.shape, gm, gn)
        spec = pl.BlockSpec((slab, cols), lambda i, j: (i * gn + j, 0))
        in_specs.append(spec)
        out_specs.append(spec)
        out_shape.append(jax.ShapeDtypeStruct(w.shape, BF16))
    res = pl.pallas_call(
        functools.partial(_mm_kernel, epilogue, len(extra), len(cast)),
        grid=(gm, gn),
        in_specs=in_specs,
        out_specs=out_specs,
        out_shape=out_shape,
        compiler_params=_params("parallel", "parallel"),
        name=name,
    )(a, b, *[x for x, _, _ in extra], *cast)
    return res


def _slab_ok(w, steps):
    return w.shape[0] % steps == 0 and (w.shape[0] // steps) % BF16_SUBLANES == 0


def _sumsq(x):
    return jnp.sum(x * x, axis=-1, keepdims=True)


def _swap_rope_halves(x):
    half = ROPE_DIM // 2
    lane = lax.broadcasted_iota(jnp.int32, x.shape, 1)
    return jnp.where(lane < half, pltpu.roll(x, LANES - half, 1), pltpu.roll(x, half, 1))


def _ep_colscale(acc, extra, outs):
    (scale_ref,) = extra
    (o_ref,) = outs
    o_ref[...] = (acc * scale_ref[...]).astype(o_ref.dtype)


def _ep_sigmoid(acc, extra, outs):
    (o_ref,) = outs
    o_ref[...] = (1.0 / (1.0 + jnp.exp(-acc))).astype(o_ref.dtype)


def _ep_latent(acc, extra, outs):
    pos_ref, freq_ref, cmask_ref, smask_ref, gq_ref, gkv_ref = extra
    cq_ref, ckv_ref, krot_ref, cos_ref, sin_ref = outs
    cq = acc[:, :Q_LORA]
    inv = lax.rsqrt(_sumsq(cq) * (1.0 / Q_LORA) + EPS)
    cq_ref[...] = (cq * inv * gq_ref[...]).astype(cq_ref.dtype)
    ckv = acc[:, Q_LORA:Q_LORA + KV_LORA]
    inv = lax.rsqrt(_sumsq(ckv) * (1.0 / KV_LORA) + EPS)
    ckv_ref[...] = (ckv * inv * gkv_ref[...]).astype(ckv_ref.dtype)
    ang = pos_ref[...].astype(F32) * freq_ref[...]
    cos_t = jnp.cos(ang) * cmask_ref[...]
    sin_t = jnp.sin(ang) * smask_ref[...]
    base = Q_LORA + KV_LORA
    ka = acc[:, base:base + LANES]
    kb = _swap_rope_halves(ka)
    krot_ref[...] = ka * cos_t + kb * sin_t
    cos_ref[...] = cos_t
    sin_ref[...] = sin_t


def _ep_mla_q(scale, acc, extra, outs):
    cos_ref, sin_ref, gn_ref, gr_ref, bias_ref = extra
    (o_ref,) = outs
    nope = acc[:, :LANES]
    ra = acc[:, LANES:]
    rot = ra * cos_ref[...] + _swap_rope_halves(ra) * sin_ref[...]
    inv = lax.rsqrt((_sumsq(nope) + _sumsq(rot)) * (1.0 / QK_DIM) + EPS) * scale
    o_ref[:, :LANES] = (nope * inv * gn_ref[...]).astype(o_ref.dtype)
    o_ref[:, LANES:] = (rot * inv * gr_ref[...] + bias_ref[...]).astype(o_ref.dtype)


def _ep_mla_kv(acc, extra, outs):
    krot_ref, gn_ref, gr_ref, bias_ref = extra
    k_ref, v_ref = outs
    nope = acc[:, :LANES]
    rot = krot_ref[...]
    inv = lax.rsqrt((_sumsq(nope) + _sumsq(rot)) * (1.0 / QK_DIM) + EPS)
    k_ref[:, :LANES] = (nope * inv * gn_ref[...]).astype(k_ref.dtype)
    k_ref[:, LANES:] = (rot * inv * gr_ref[...] + bias_ref[...]).astype(k_ref.dtype)
    v_ref[:, :LANES] = acc[:, LANES:].astype(v_ref.dtype)
    v_ref[:, LANES:] = jnp.ones((acc.shape[0], LANES), v_ref.dtype)


def _ep_mem_attn(acc, extra, outs):
    mk_ref, mv_ref, gq_ref = extra
    (o_ref,) = outs
    scale = 1.0 / np.sqrt(MEM_HEAD_DIM)
    for h in range(MEM_HEADS):
        sl = slice(h * MEM_HEAD_DIM, (h + 1) * MEM_HEAD_DIM)
        qh = acc[:, sl]
        inv = lax.rsqrt(_sumsq(qh) * (1.0 / MEM_HEAD_DIM) + EPS) * scale
        qn = (qh * inv * gq_ref[...]).astype(BF16)
        s = lax.dot_general(qn, mk_ref[:, sl], (((1,), (1,)), ((), ())), preferred_element_type=F32)
        p = jnp.exp(s - jnp.max(s, axis=-1, keepdims=True))
        den = jnp.sum(p, axis=-1, keepdims=True)
        o = jnp.dot(p.astype(BF16), mv_ref[:, sl], preferred_element_type=F32)
        o_ref[:, sl] = (o / den).astype(o_ref.dtype)


def _ep_relu2(acc, extra, outs):
    (o_ref,) = outs
    u = jnp.maximum(acc, 0.0)
    o_ref[...] = (u * u).astype(o_ref.dtype)


def _ep_residual(acc, extra, outs):
    (r_ref,) = extra
    (o_ref,) = outs
    o_ref[...] = r_ref[...] + acc


def _memkv_kernel(mem_ref, g_ref, w_ref, gk_ref, o_ref):
    x = mem_ref[...]
    ms = jnp.mean(x * x, axis=-1, keepdims=True)
    hn = (x * lax.rsqrt(ms + EPS) * g_ref[...]).astype(BF16)
    acc = jnp.dot(hn, w_ref[...], preferred_element_type=F32)
    j = pl.program_id(0)
    inv = lax.rsqrt(_sumsq(acc) * (1.0 / MEM_HEAD_DIM) + EPS)
    normed = acc * inv * gk_ref[...]
    o_ref[...] = jnp.where(j < MEM_HEADS, normed, acc).astype(o_ref.dtype)


def _memkv(mem, g_mem, w_mem_kv_bf16, g_k_mem):
    n_mem, d = mem.shape
    n = w_mem_kv_bf16.shape[1]
    return pl.pallas_call(
        _memkv_kernel,
        grid=(n // MEM_HEAD_DIM,),
        in_specs=[
            pl.BlockSpec((n_mem, d), lambda j: (0, 0)),
            pl.BlockSpec((1, d), lambda j: (0, 0)),
            pl.BlockSpec((d, MEM_HEAD_DIM), lambda j: (0, j)),
            pl.BlockSpec((1, MEM_HEAD_DIM), lambda j: (0, 0)),
        ],
        out_specs=pl.BlockSpec((n_mem, MEM_HEAD_DIM), lambda j: (0, j)),
        out_shape=jax.ShapeDtypeStruct((n_mem, n), BF16),
        compiler_params=_params("parallel"),
        name="mem_kv",
    )(mem, g_mem.reshape(1, d), w_mem_kv_bf16, g_k_mem.reshape(1, MEM_HEAD_DIM))


def _sb_attn_kernel(q_ref, k_ref, v_ref, tri_win_ref, tri_ref, o_ref, acc_ref, carry_ref):
    t = q_ref.shape[0]
    i = pl.program_id(1)
    q = q_ref[...]

    def block(start, width, tri_ref, mask_offset):
        k = k_ref[pl.ds(start, width), :]
        v = v_ref[pl.ds(start, width), :]
        nz = lax.dot_general(q, k, (((1,), (1,)), ((), ())), preferred_element_type=F32)
        l = jnp.minimum(nz, 0.0) - jnp.log(1.0 + jnp.exp(-jnp.abs(nz)))
        if mask_offset is not None:
            row = lax.broadcasted_iota(jnp.int32, (t, width), 0)
            col = lax.broadcasted_iota(jnp.int32, (t, width), 1)
            causal = col < row + mask_offset
            l = jnp.where(causal, l, 0.0)
        l_hi = l.astype(BF16)
        l_lo = (l - l_hi.astype(F32)).astype(BF16)
        suffix = jnp.dot(jnp.concatenate([l_hi, l_lo], axis=1), tri_ref[...], preferred_element_type=F32)
        carry = carry_ref[...]
        a = jnp.exp((l - nz) + suffix + jnp.concatenate([carry] * (width // LANES), axis=1))
        if mask_offset is not None:
            a = jnp.where(causal, a, 0.0)
        acc_ref[...] += jnp.dot(a.astype(BF16), v, preferred_element_type=F32)
        new_carry = carry + jnp.broadcast_to(suffix[:, 0:1] + l[:, 0:1], carry.shape)
        carry_ref[...] = new_carry
        return jnp.max(new_carry)

    acc_ref[...] = jnp.zeros_like(acc_ref)
    carry_ref[...] = jnp.zeros_like(carry_ref)
    first = jnp.maximum(i - 1, 0)
    top = block(pl.multiple_of(first * t, t), 2 * t, tri_win_ref, (i - first) * t)

    def cond(state):
        j, top = state
        return jnp.logical_and(j >= 0, top > SB_DEAD_LOG)

    def body(state):
        j, _ = state
        return j - 1, block(pl.multiple_of(j * t, t), t, tri_ref, None)

    lax.while_loop(cond, body, (i - 2, top))
    o_ref[...] = acc_ref[...].astype(o_ref.dtype)


def _suffix_ones(width):
    tri = (np.arange(width)[:, None] > np.arange(width)[None, :]).astype(np.float32)
    return jnp.asarray(np.concatenate([tri, tri], axis=0), dtype=BF16)


def _sb_attention(qkv):
    s = qkv.shape[0]
    t = _tile(s, ATTN_TILE)
    assert s >= 2 * t
    hd = SB_HEAD_DIM
    return pl.pallas_call(
        _sb_attn_kernel,
        grid=(SB_HEADS, s // t),
        in_specs=[
            pl.BlockSpec((t, hd), lambda h, i: (i, h)),
            pl.BlockSpec((s, hd), lambda h, i: (0, SB_HEADS + h)),
            pl.BlockSpec((s, hd), lambda h, i: (0, 2 * SB_HEADS + h)),
            pl.BlockSpec((4 * t, 2 * t), lambda h, i: (0, 0)),
            pl.BlockSpec((2 * t, t), lambda h, i: (0, 0)),
        ],
        out_specs=pl.BlockSpec((t, hd), lambda h, i: (i, h)),
        out_shape=jax.ShapeDtypeStruct((s, SB_W), BF16),
        scratch_shapes=[pltpu.VMEM((t, hd), F32), pltpu.VMEM((t, LANES), F32)],
        compiler_params=_params("parallel", "parallel"),
        name="sb_attention",
    )(qkv, qkv, qkv, _suffix_ones(2 * t), _suffix_ones(t))


def _mla_attn_kernel(fast_ref, q_ref, k_ref, v_ref, o_ref, acc_ref, m_ref):
    tq = q_ref.shape[0]
    t = tq // 2
    i = pl.program_id(1)
    q = q_ref[...]

    def scores(j, chunk_offset):
        start = pl.multiple_of(j * t, t)
        s = lax.dot_general(q, k_ref[pl.ds(start, t), :], (((1,), (1,)), ((), ())), preferred_element_type=F32)
        if chunk_offset is not None:
            row = lax.broadcasted_iota(jnp.int32, (tq, t), 0)
            col = lax.broadcasted_iota(jnp.int32, (tq, t), 1)
            s = jnp.where((col // CHUNK) + chunk_offset <= (row // CHUNK), s, -jnp.inf)
        return s, v_ref[pl.ds(start, t), :]

    def shifted_tile(j, chunk_offset):
        s, v = scores(j, chunk_offset)
        acc_ref[...] += jnp.dot(jnp.exp2(s).astype(BF16), v, preferred_element_type=F32)

    def online_tile(j, chunk_offset):
        s, v = scores(j, chunk_offset)
        m_prev = m_ref[...]
        m_new = jnp.maximum(m_prev, jnp.max(s, axis=-1, keepdims=True))
        alpha = jnp.exp2(m_prev - m_new)
        p = jnp.exp2(s - jnp.concatenate([m_new] * (t // LANES), axis=1))
        acc_ref[...] = jnp.concatenate([alpha, alpha], axis=1) * acc_ref[...] + jnp.dot(
            p.astype(BF16), v, preferred_element_type=F32)
        m_ref[...] = m_new

    def sweep(tile):
        tile(2 * i, 0)
        tile(2 * i + 1, t // CHUNK)

        def pair(p, c):
            tile(2 * p, None)
            tile(2 * p + 1, None)
            return c

        lax.fori_loop(0, i, pair, 0)

    acc_ref[...] = jnp.zeros_like(acc_ref)
    shifted = fast_ref[0] == 1

    @pl.when(shifted)
    def _():
        sweep(shifted_tile)

    @pl.when(jnp.logical_not(shifted))
    def _():
        m_ref[...] = jnp.full_like(m_ref, -jnp.inf)
        sweep(online_tile)

    acc = acc_ref[...]
    o_ref[...] = (acc[:, :V_DIM] / acc[:, V_DIM:]).astype(o_ref.dtype)


def _mla_attention(shifted, q, k, v_ones):
    s = q.shape[0]
    t = _tile(s, 2 * MLA_TILE)
    dk = 2 * LANES
    return pl.pallas_call(
        _mla_attn_kernel,
        grid_spec=pltpu.PrefetchScalarGridSpec(
            num_scalar_prefetch=1,
            grid=(MLA_HEADS, s // t),
            in_specs=[
                pl.BlockSpec((t, dk), lambda h, i, f: (i, h)),
                pl.BlockSpec((s, dk), lambda h, i, f: (0, h)),
                pl.BlockSpec((s, dk), lambda h, i, f: (0, h)),
            ],
            out_specs=pl.BlockSpec((t, V_DIM), lambda h, i, f: (i, h)),
            scratch_shapes=[pltpu.VMEM((t, 2 * V_DIM), F32), pltpu.VMEM((t, LANES), F32)],
        ),
        out_shape=jax.ShapeDtypeStruct((s, MLA_HEADS * V_DIM), BF16),
        compiler_params=_params("parallel", "parallel"),
        name="mla_attention",
    )(shifted, q, k, v_ones)


def _merge_kernel(osb_ref, omla_ref, omem_ref, wsb_ref, wmla_ref, wmem_ref, g0_ref, g1_ref, g2_ref, o_ref):
    acc = g0_ref[...].astype(F32) * jnp.dot(osb_ref[...], wsb_ref[...], preferred_element_type=F32)
    acc += g1_ref[...].astype(F32) * jnp.dot(omla_ref[...], wmla_ref[...], preferred_element_type=F32)
    acc += g2_ref[...].astype(F32) * jnp.dot(omem_ref[...], wmem_ref[...], preferred_element_type=F32)
    o_ref[...] = acc.astype(o_ref.dtype)


def _merge(o_sb, o_mla, o_mem, w_sb_o, w_mla_o, w_mem_o, gates):
    m = o_sb.shape[0]
    d = w_sb_o.shape[1]
    tm = _tile(m, 1024)
    tn = _tile(d, 1024)
    nj = d // tn
    row = lambda i, j: (i, 0)
    colw = lambda i, j: (0, j)
    return pl.pallas_call(
        _merge_kernel,
        grid=(m // tm, nj),
        in_specs=[
            pl.BlockSpec((tm, o_sb.shape[1]), row),
            pl.BlockSpec((tm, o_mla.shape[1]), row),
            pl.BlockSpec((tm, o_mem.shape[1]), row),
            pl.BlockSpec((w_sb_o.shape[0], tn), colw),
            pl.BlockSpec((w_mla_o.shape[0], tn), colw),
            pl.BlockSpec((w_mem_o.shape[0], tn), colw),
            pl.BlockSpec((tm, tn), lambda i, j: (i, j)),
            pl.BlockSpec((tm, tn), lambda i, j: (i, nj + j)),
            pl.BlockSpec((tm, tn), lambda i, j: (i, 2 * nj + j)),
        ],
        out_specs=pl.BlockSpec((tm, tn), lambda i, j: (i, j)),
        out_shape=jax.ShapeDtypeStruct((m, d), BF16),
        compiler_params=_params("parallel", "parallel"),
        name="gated_merge",
    )(o_sb, o_mla, o_mem, w_sb_o, w_mla_o, w_mem_o, gates, gates, gates)


def _mm_kacc_kernel(a_ref, b_ref, r_ref, o_ref):
    @pl.when(pl.program_id(2) == 0)
    def _():
        o_ref[...] = r_ref[...]

    o_ref[...] += jnp.dot(a_ref[...], b_ref[...], preferred_element_type=F32)


def _matmul_kacc_residual(a, b, r, name):
    m, k = a.shape
    n = b.shape[1]
    tm, tn, tk = _tile(m, 1024), _tile(n, 1024), _tile(k, 2048)
    return pl.pallas_call(
        _mm_kacc_kernel,
        grid=(m // tm, n // tn, k // tk),
        in_specs=[
            pl.BlockSpec((tm, tk), lambda i, j, kk: (i, kk)),
            pl.BlockSpec((tk, tn), lambda i, j, kk: (kk, j)),
            pl.BlockSpec((tm, tn), lambda i, j, kk: (i, j)),
        ],
        out_specs=pl.BlockSpec((tm, tn), lambda i, j, kk: (i, j)),
        out_shape=jax.ShapeDtypeStruct((m, n), F32),
        compiler_params=_params("parallel", "parallel", "arbitrary"),
        name=name,
    )(a, b, r)


def _rope_tables():
    half = ROPE_DIM // 2
    freqs = 1.0 / (ROPE_THETA ** (jnp.arange(half, dtype=F32) / half))
    zeros = jnp.zeros((LANES - ROPE_DIM,), F32)
    freq = jnp.concatenate([freqs, freqs, zeros]).reshape(1, LANES)
    cmask = jnp.concatenate([jnp.ones((ROPE_DIM,), F32), zeros]).reshape(1, LANES)
    smask = jnp.concatenate([-jnp.ones((half,), F32), jnp.ones((half,), F32), zeros]).reshape(1, LANES)
    return freq, cmask, smask


def _pad_gain(g):
    return jnp.concatenate([g, jnp.zeros((LANES - g.shape[0],), g.dtype)]).reshape(1, LANES)


def _layer(x, mem, positions, g_mix, g_mem, w_in, g_cq, g_ckv, w_q_b, w_kv_b, g_q_mla, g_k_mla, w_mem_kv,
           g_q_mem, g_k_mem, w_sb_o, w_mla_o, w_mem_o, w_out, g_ffn, w_ff1, w_ff2):
    s, d = x.shape
    pos = positions.reshape(s, 1)

    c0 = 3 * SB_W
    c1 = c0 + Q_LORA
    c2 = c1 + KV_LORA
    c3 = c2 + ROPE_DIM
    c4 = c3 + MEM_W
    tng = _tile(3 * d, 1024)
    c3p = c3 + LANES - ROPE_DIM
    gate_pad = -(c3p + MEM_W) % tng
    c4p = c3p + MEM_W + gate_pad
    w_pad = _cast_pad_w_in(w_in, [(c3, LANES - ROPE_DIM)] + ([(c4, gate_pad)] if gate_pad else []))
    wq = w_q_b.reshape(Q_LORA, MLA_HEADS, QK_DIM)
    wq_pad = jnp.zeros((Q_LORA, MLA_HEADS, LANES - ROPE_DIM), wq.dtype)
    w_qb = jnp.concatenate([wq, wq_pad], axis=-1).reshape(Q_LORA, MLA_HEADS * 2 * LANES).astype(BF16)
    w_kvb = w_kv_b.astype(BF16)

    bound = np.float32(np.sqrt(QK_DIM) * np.log2(np.e)) * jnp.max(jnp.abs(g_q_mla)) * jnp.max(jnp.abs(g_k_mla))
    shifted = bound <= MLA_SHIFT_BOUND
    shift_lane = (jnp.arange(LANES) == ROPE_DIM).astype(F32).reshape(1, LANES)
    q_bias = shift_lane * jnp.where(shifted, -bound, 0.0)
    k_bias = shift_lane

    h = _rmsnorm(x, g_mix, "rmsnorm_mix")

    tm = _tile(s, 1024)

    sb_scale = -1.0 / np.sqrt(SB_HEAD_DIM).astype(np.float32)
    colscale = jnp.concatenate([jnp.full((SB_W,), sb_scale, F32), jnp.ones((2 * SB_W,), F32)]).reshape(1, c0)
    tn = _tile(c0, 768)
    (sb_qkv,) = _matmul(
        h, w_pad, n=c0, tm=tm, tn=tn,
        extra=[(colscale, (1, tn), lambda i, j: (0, j))],
        outs=[((s, c0), BF16, (tm, tn), lambda i, j: (i, j))],
        epilogue=_ep_colscale, name="proj_sb_qkv")
    o_sb = _sb_attention(sb_qkv)

    freq, cmask, smask = _rope_tables()
    tml = _tile(s, 512)
    nlat = c3p - c0
    full = lambda shape: (shape, lambda i, j: (0, 0))
    rows = lambda w: ((tml, w), lambda i, j: (i, 0))
    side = [w_out] if _slab_ok(w_out, s // tml) else []
    cq_n, ckv_n, krot, cos_t, sin_t, *casted = _matmul(
        h, w_pad, b_col0=c0, n=nlat, tm=tml, tn=nlat, cast=side,
        extra=[(pos, *rows(1)), (freq, *full((1, LANES))), (cmask, *full((1, LANES))), (smask, *full((1, LANES))),
               (g_cq.reshape(1, Q_LORA), *full((1, Q_LORA))), (g_ckv.reshape(1, KV_LORA), *full((1, KV_LORA)))],
        outs=[((s, Q_LORA), BF16, *rows(Q_LORA)), ((s, KV_LORA), BF16, *rows(KV_LORA)),
              ((s, LANES), F32, *rows(LANES)), ((s, LANES), F32, *rows(LANES)), ((s, LANES), F32, *rows(LANES))],
        epilogue=_ep_latent, name="proj_mla_latent")
    w_out_b = casted[0] if side else w_out.astype(BF16)

    gq_n, gq_r = g_q_mla[:NOPE_DIM].reshape(1, LANES), _pad_gain(g_q_mla[NOPE_DIM:])
    gk_n, gk_r = g_k_mla[:NOPE_DIM].reshape(1, LANES), _pad_gain(g_k_mla[NOPE_DIM:])
    rows = lambda w: ((tm, w), lambda i, j: (i, 0))
    mla_scale = np.float32(np.log2(np.e) / np.sqrt(QK_DIM))
    (q_mla,) = _matmul(
        cq_n, w_qb, tm=tm, tn=2 * LANES,
        extra=[(cos_t, *rows(LANES)), (sin_t, *rows(LANES)), (gq_n, *full((1, LANES))), (gq_r, *full((1, LANES))),
               (q_bias, *full((1, LANES)))],
        outs=[((s, MLA_HEADS * 2 * LANES), BF16, (tm, 2 * LANES), lambda i, j: (i, j))],
        epilogue=functools.partial(_ep_mla_q, mla_scale), name="mla_q_up")
    k_mla, v_mla = _matmul(
        ckv_n, w_kvb, tm=tm, tn=2 * LANES,
        extra=[(krot, *rows(LANES)), (gk_n, *full((1, LANES))), (gk_r, *full((1, LANES))),
               (k_bias, *full((1, LANES)))],
        outs=[((s, MLA_HEADS * 2 * LANES), BF16, (tm, 2 * LANES), lambda i, j: (i, j)),
              ((s, MLA_HEADS * 2 * LANES), BF16, (tm, 2 * LANES), lambda i, j: (i, j))],
        epilogue=_ep_mla_kv, name="mla_kv_up")
    o_mla = _mla_attention(shifted.astype(jnp.int32).reshape(1), q_mla, k_mla, v_mla)

    mkv = _memkv(mem, g_mem, w_mem_kv.astype(BF16), g_k_mem)
    n_mem = mem.shape[0]
    tmm = _tile(s, 512)
    branch_w = [w_sb_o, w_mla_o, w_mem_o]
    side = branch_w if all(_slab_ok(w, s // tmm) for w in branch_w) else []
    o_mem, *casted = _matmul(
        h, w_pad, b_col0=c3p, n=MEM_W, tm=tmm, tn=MEM_W, cast=side,
        extra=[(mkv, (n_mem, MEM_W), lambda i, j: (0, 0)), (mkv, (n_mem, MEM_W), lambda i, j: (0, 1)),
               (g_q_mem.reshape(1, MEM_HEAD_DIM), (1, MEM_HEAD_DIM), lambda i, j: (0, 0))],
        outs=[((s, MEM_W), BF16, (tmm, MEM_W), lambda i, j: (i, 0))],
        epilogue=_ep_mem_attn, name="mem_attention")
    w_sb_o_b, w_mla_o_b, w_mem_o_b = casted if side else [w.astype(BF16) for w in branch_w]

    (gates,) = _matmul(
        h, w_pad, b_col0=c4p, n=3 * d, tm=tm, tn=tng,
        outs=[((s, 3 * d), BF16, (tm, tng), lambda i, j: (i, j))],
        epilogue=_ep_sigmoid, name="proj_gates")
    merged = _merge(o_sb, o_mla, o_mem, w_sb_o_b, w_mla_o_b, w_mem_o_b, gates)
    tnd = _tile(d, 512)
    side = [w_ff1] if _slab_ok(w_ff1, (s // tm) * (d // tnd)) else []
    x1, *casted = _matmul(
        merged, w_out_b, tm=tm, tn=tnd, cast=side,
        extra=[(x, (tm, tnd), lambda i, j: (i, j))],
        outs=[((s, d), F32, (tm, tnd), lambda i, j: (i, j))],
        epilogue=_ep_residual, name="out_proj")
    w_ff1_b = casted[0] if side else w_ff1.astype(BF16)

    hn = _rmsnorm(x1, g_ffn, "rmsnorm_ffn")
    dff = w_ff1.shape[1]
    tnf = _tile(dff, 1024)
    side = [w_ff2] if _slab_ok(w_ff2, (s // tm) * (dff // tnf)) else []
    u2, *casted = _matmul(
        hn, w_ff1_b, tm=tm, tn=tnf, cast=side,
        outs=[((s, dff), BF16, (tm, tnf), lambda i, j: (i, j))],
        epilogue=_ep_relu2, name="ffn_up")
    w_ff2_b = casted[0] if side else w_ff2.astype(BF16)
    return _matmul_kacc_residual(u2, w_ff2_b, x1, "ffn_down")


def kernel(x, mem, positions, g_mix, g_mem, w_in, g_cq, g_ckv, w_q_b, w_kv_b, g_q_mla, g_k_mla, w_mem_kv, g_q_mem,
           g_k_mem, w_sb_o, w_mla_o, w_mem_o, w_out, g_ffn, w_ff1, w_ff2):
    depth = w_in.shape[0]
    assert x.shape[0] == 1 and mem.shape[0] == 1
    y = x[0]
    for i in range(depth):
        y = _layer(y, mem[0], positions[0], g_mix[i], g_mem[i], w_in[i], g_cq[i], g_ckv[i], w_q_b[i], w_kv_b[i],
                   g_q_mla[i], g_k_mla[i], w_mem_kv[i], g_q_mem[i], g_k_mem[i], w_sb_o[i], w_mla_o[i], w_mem_o[i],
                   w_out[i], g_ffn[i], w_ff1[i], w_ff2[i])
    return y[None]
```

```python
import functools

import jax
import jax.numpy as jnp
import numpy as np
from jax import lax
from jax.experimental import pallas as pl
from jax.experimental.pallas import tpu as pltpu

F32 = jnp.float32
BF16 = jnp.bfloat16

CHUNK = 64
SB_HEADS = 12
SB_HEAD_DIM = 128
SB_W = SB_HEADS * SB_HEAD_DIM
MLA_HEADS = 12
Q_LORA = 896
KV_LORA = 512
NOPE_DIM = 128
ROPE_DIM = 64
V_DIM = 128
QK_DIM = NOPE_DIM + ROPE_DIM
MEM_HEADS = 4
MEM_HEAD_DIM = 256
MEM_W = MEM_HEADS * MEM_HEAD_DIM
ROPE_THETA = 10000.0
EPS = 1e-6

LANES = 128
BF16_SUBLANES = 16
VMEM_LIMIT_BYTES = 56 * 1024 * 1024
SB_DEAD_LOG = -120.0
ATTN_TILE = 256
SB_HEADS_PER_STEP = 4
MLA_TILE = 512
MLA_SHIFT_BOUND = 50.0
MLA_HEADS_PER_STEP = 4


def _params(*sem):
    return pltpu.CompilerParams(dimension_semantics=sem, vmem_limit_bytes=VMEM_LIMIT_BYTES)


def _tile(n, pref):
    t = int(min(n, pref))
    assert n % t == 0, (n, t)
    return t


def _rmsnorm_kernel(x_ref, g_ref, o_ref):
    x = x_ref[...]
    ms = jnp.mean(x * x, axis=-1, keepdims=True)
    o_ref[...] = (x * lax.rsqrt(ms + EPS) * g_ref[...]).astype(o_ref.dtype)


def _rmsnorm(x, g, name):
    m, d = x.shape
    tm = _tile(m, 256)
    return pl.pallas_call(
        _rmsnorm_kernel,
        grid=(m // tm,),
        in_specs=[pl.BlockSpec((tm, d), lambda i: (i, 0)), pl.BlockSpec((1, d), lambda i: (0, 0))],
        out_specs=pl.BlockSpec((tm, d), lambda i: (i, 0)),
        out_shape=jax.ShapeDtypeStruct((m, d), BF16),
        compiler_params=_params("parallel"),
        name=name,
    )(x, g.reshape(1, d))


def _cast_kernel(w_ref, o_ref):
    o_ref[...] = w_ref[...].astype(o_ref.dtype)


def _cast_bf16(w, name):
    rows, cols = w.shape
    assert rows % BF16_SUBLANES == 0
    units = rows // BF16_SUBLANES
    tr = BF16_SUBLANES * max(u for u in range(1, 129) if units % u == 0)
    tc = _tile(cols, 1024)
    spec = pl.BlockSpec((tr, tc), lambda i, j: (i, j))
    return pl.pallas_call(
        _cast_kernel,
        grid=(rows // tr, cols // tc),
        in_specs=[spec],
        out_specs=spec,
        out_shape=jax.ShapeDtypeStruct((rows, cols), BF16),
        compiler_params=_params("parallel", "parallel"),
        name=name,
    )(w)


def _mm_kernel(epilogue, n_extra, n_cast, b_transposed, a_ref, b_ref, *refs):
    n_out = len(refs) - n_extra - 2 * n_cast
    extra = refs[:n_extra]
    cast_in = refs[n_extra:n_extra + n_cast]
    outs = refs[n_extra + n_cast:n_extra + n_cast + n_out]
    cast_out = refs[n_extra + n_cast + n_out:]
    if b_transposed:
        acc = lax.dot_general(a_ref[...], b_ref[...], (((1,), (1,)), ((), ())), preferred_element_type=F32)
    else:
        acc = jnp.dot(a_ref[...], b_ref[...], preferred_element_type=F32)
    epilogue(acc, extra, outs)
    for src_ref, dst_ref in zip(cast_in, cast_out):
        dst_ref[...] = src_ref[...].astype(dst_ref.dtype)


def _matmul(a, b, *, tm, tn, extra=(), outs, epilogue, name, b_col0=0, n=None, cast=(), b_transposed=False):
    m, k = a.shape
    k2, n_total = (b.shape[1], b.shape[0]) if b_transposed else b.shape
    n = n_total if n is None else n
    assert k == k2 and m % tm == 0 and n % tn == 0 and b_col0 + n <= n_total
    gm, gn = m // tm, n // tn
    if b_transposed:
        assert b_col0 % BF16_SUBLANES == 0 and tn % BF16_SUBLANES == 0
        b_spec = pl.BlockSpec((pl.Element(tn), pl.Element(k)),
                              lambda i, j: (pl.multiple_of(b_col0 + j * tn, BF16_SUBLANES), 0))
    else:
        assert b_col0 % tn == 0
        jb = b_col0 // tn
        b_spec = pl.BlockSpec((k, tn), lambda i, j: (0, jb + j))
    in_specs = [pl.BlockSpec((tm, k), lambda i, j: (i, 0)), b_spec]
    in_specs += [pl.BlockSpec(bs, im) for _, bs, im in extra]
    out_specs = [pl.BlockSpec(bs, im) for _, _, bs, im in outs]
    out_shape = [jax.ShapeDtypeStruct(s, dt) for s, dt, _, _ in outs]
    for w in cast:
        rows, cols = w.shape
        slab = rows // (gm * gn)
        assert slab * gm * gn == rows and slab % BF16_SUBLANES == 0, (name, w.shape, gm, gn)
        spec = pl.BlockSpec((slab, cols), lambda i, j: (i * gn + j, 0))
        in_specs.append(spec)
        out_specs.append(spec)
        out_shape.append(jax.ShapeDtypeStruct(w.shape, BF16))
    res = pl.pallas_call(
        functools.partial(_mm_kernel, epilogue, len(extra), len(cast), b_transposed),
        grid=(gm, gn),
        in_specs=in_specs,
        out_specs=out_specs,
        out_shape=out_shape,
        compiler_params=_params("parallel", "parallel"),
        name=name,
    )(a, b, *[x for x, _, _ in extra], *cast)
    return res


def _slab_ok(w, steps):
    return w.shape[0] % steps == 0 and (w.shape[0] // steps) % BF16_SUBLANES == 0


def _sumsq(x):
    return jnp.sum(x * x, axis=-1, keepdims=True)


def _sumsq2(x, y):
    return jnp.sum(x * x + y * y, axis=-1, keepdims=True)


def _swap_rope_halves(x):
    half = ROPE_DIM // 2
    lane = lax.broadcasted_iota(jnp.int32, x.shape, 1)
    return jnp.where(lane < half, pltpu.roll(x, LANES - half, 1), pltpu.roll(x, half, 1))


def _ep_colscale(acc, extra, outs):
    (scale_ref,) = extra
    (o_ref,) = outs
    o_ref[...] = (acc * scale_ref[...]).astype(o_ref.dtype)


def _ep_sigmoid(acc, extra, outs):
    (o_ref,) = outs
    o_ref[...] = (1.0 / (1.0 + jnp.exp(-acc))).astype(o_ref.dtype)


def _ep_latent(acc, extra, outs):
    pos_ref, freq_ref, cmask_ref, smask_ref, gq_ref, gkv_ref = extra
    cq_ref, ckv_ref, krot_ref, cos_ref, sin_ref = outs
    cq = acc[:, :Q_LORA]
    inv = lax.rsqrt(_sumsq(cq) * (1.0 / Q_LORA) + EPS)
    cq_ref[...] = (cq * inv * gq_ref[...]).astype(cq_ref.dtype)
    ckv = acc[:, Q_LORA:Q_LORA + KV_LORA]
    inv = lax.rsqrt(_sumsq(ckv) * (1.0 / KV_LORA) + EPS)
    ckv_ref[...] = (ckv * inv * gkv_ref[...]).astype(ckv_ref.dtype)
    ang = pos_ref[...].astype(F32) * freq_ref[...]
    cos_t = jnp.cos(ang) * cmask_ref[...]
    sin_t = jnp.sin(ang) * smask_ref[...]
    base = Q_LORA + KV_LORA
    ka = acc[:, base:base + LANES]
    kb = _swap_rope_halves(ka)
    krot_ref[...] = ka * cos_t + kb * sin_t
    cos_ref[...] = cos_t
    sin_ref[...] = sin_t


def _ep_mla_q(scale, acc, extra, outs):
    cos_ref, sin_ref, gn_ref, gr_ref, bias_ref = extra
    (o_ref,) = outs
    for hh in range(acc.shape[1] // (3 * LANES)):
        src, dst = hh * 3 * LANES, hh * 2 * LANES
        nope = acc[:, src:src + LANES]
        rot = acc[:, src + LANES:src + 2 * LANES] * cos_ref[...] + acc[:, src + 2 * LANES:src + 3 * LANES] * sin_ref[...]
        inv = lax.rsqrt(_sumsq2(nope, rot) * (1.0 / QK_DIM) + EPS) * scale
        o_ref[:, dst:dst + LANES] = (nope * inv * gn_ref[...]).astype(o_ref.dtype)
        o_ref[:, dst + LANES:dst + 2 * LANES] = (rot * inv * gr_ref[...] + bias_ref[...]).astype(o_ref.dtype)


def _ep_mla_kv(acc, extra, outs):
    krot_ref, gn_ref, gr_ref, bias_ref = extra
    k_ref, v_ref = outs
    rot = krot_ref[...]
    for base in range(0, acc.shape[1], 2 * LANES):
        nope = acc[:, base:base + LANES]
        inv = lax.rsqrt(_sumsq2(nope, rot) * (1.0 / QK_DIM) + EPS)
        k_ref[:, base:base + LANES] = (nope * inv * gn_ref[...]).astype(k_ref.dtype)
        k_ref[:, base + LANES:base + 2 * LANES] = (rot * inv * gr_ref[...] + bias_ref[...]).astype(k_ref.dtype)
        v_ref[:, base:base + LANES] = acc[:, base + LANES:base + 2 * LANES].astype(v_ref.dtype)
        v_ref[:, base + LANES:base + 2 * LANES] = jnp.ones((acc.shape[0], LANES), v_ref.dtype)


def _ep_mem_attn(acc, extra, outs):
    mk_ref, mv_ref, gq_ref = extra
    (o_ref,) = outs
    scale = 1.0 / np.sqrt(MEM_HEAD_DIM)
    for h in range(MEM_HEADS):
        sl = slice(h * MEM_HEAD_DIM, (h + 1) * MEM_HEAD_DIM)
        qh = acc[:, sl]
        inv = lax.rsqrt(_sumsq(qh) * (1.0 / MEM_HEAD_DIM) + EPS) * scale
        qn = (qh * inv * gq_ref[...]).astype(BF16)
        s = lax.dot_general(qn, mk_ref[:, sl], (((1,), (1,)), ((), ())), preferred_element_type=F32)
        p = jnp.exp(s - jnp.max(s, axis=-1, keepdims=True))
        den = jnp.sum(p, axis=-1, keepdims=True)
        o = jnp.dot(p.astype(BF16), mv_ref[:, sl], preferred_element_type=F32)
        o_ref[:, sl] = (o / den).astype(o_ref.dtype)


def _ep_relu2(acc, extra, outs):
    (o_ref,) = outs
    u = jnp.maximum(acc, 0.0)
    o_ref[...] = (u * u).astype(o_ref.dtype)


def _ep_residual(acc, extra, outs):
    (r_ref,) = extra
    (o_ref,) = outs
    o_ref[...] = r_ref[...] + acc


def _memkv_kernel(mem_ref, g_ref, w_ref, gk_ref, o_ref):
    x = mem_ref[...]
    ms = jnp.mean(x * x, axis=-1, keepdims=True)
    hn = (x * lax.rsqrt(ms + EPS) * g_ref[...]).astype(BF16)
    acc = jnp.dot(hn, w_ref[...], preferred_element_type=F32)
    j = pl.program_id(0)
    inv = lax.rsqrt(_sumsq(acc) * (1.0 / MEM_HEAD_DIM) + EPS)
    normed = acc * inv * gk_ref[...]
    o_ref[...] = jnp.where(j < MEM_HEADS, normed, acc).astype(o_ref.dtype)


def _memkv(mem, g_mem, w_mem_kv_bf16, g_k_mem):
    n_mem, d = mem.shape
    n = w_mem_kv_bf16.shape[1]
    return pl.pallas_call(
        _memkv_kernel,
        grid=(n // MEM_HEAD_DIM,),
        in_specs=[
            pl.BlockSpec((n_mem, d), lambda j: (0, 0)),
            pl.BlockSpec((1, d), lambda j: (0, 0)),
            pl.BlockSpec((d, MEM_HEAD_DIM), lambda j: (0, j)),
            pl.BlockSpec((1, MEM_HEAD_DIM), lambda j: (0, 0)),
        ],
        out_specs=pl.BlockSpec((n_mem, MEM_HEAD_DIM), lambda j: (0, j)),
        out_shape=jax.ShapeDtypeStruct((n_mem, n), BF16),
        compiler_params=_params("parallel"),
        name="mem_kv",
    )(mem, g_mem.reshape(1, d), w_mem_kv_bf16, g_k_mem.reshape(1, MEM_HEAD_DIM))


def _sb_attn_kernel(q_ref, k_ref, v_ref, tri_win_ref, tri_ref, o_ref, acc_ref, carry_ref):
    t = q_ref.shape[0]
    hd = SB_HEAD_DIM
    i = pl.program_id(1)

    def block(head, start, width, tri_ref, mask_offset):
        cols = slice(head * hd, (head + 1) * hd)
        k = k_ref[pl.ds(start, width), cols]
        v = v_ref[pl.ds(start, width), cols]
        nz = lax.dot_general(q_ref[:, cols], k, (((1,), (1,)), ((), ())), preferred_element_type=F32)
        l = jnp.minimum(nz, 0.0) - jnp.log(1.0 + jnp.exp(-jnp.abs(nz)))
        if mask_offset is not None:
            row = lax.broadcasted_iota(jnp.int32, (t, width), 0)
            col = lax.broadcasted_iota(jnp.int32, (t, width), 1)
            causal = col < row + mask_offset
            l = jnp.where(causal, l, 0.0)
        l_hi = l.astype(BF16)
        l_lo = (l - l_hi.astype(F32)).astype(BF16)
        suffix = jnp.dot(jnp.concatenate([l_hi, l_lo], axis=1), tri_ref[...], preferred_element_type=F32)
        carry = carry_ref[:, cols]
        a = jnp.exp((l - nz) + suffix + jnp.concatenate([carry] * (width // LANES), axis=1))
        if mask_offset is not None:
            a = jnp.where(causal, a, 0.0)
        acc_ref[:, cols] += jnp.dot(a.astype(BF16), v, preferred_element_type=F32)
        new_carry = carry + jnp.broadcast_to(suffix[:, 0:1] + l[:, 0:1], carry.shape)
        carry_ref[:, cols] = new_carry
        return jnp.max(new_carry)

    def blocks(start, width, tri_ref, mask_offset):
        tops = [block(head, start, width, tri_ref, mask_offset) for head in range(SB_HEADS_PER_STEP)]
        return functools.reduce(jnp.maximum, tops)

    acc_ref[...] = jnp.zeros_like(acc_ref)
    carry_ref[...] = jnp.zeros_like(carry_ref)
    first = jnp.maximum(i - 1, 0)
    top = blocks(pl.multiple_of(first * t, t), 2 * t, tri_win_ref, (i - first) * t)

    def cond(state):
        j, top = state
        return jnp.logical_and(j >= 0, top > SB_DEAD_LOG)

    def body(state):
        j, _ = state
        return j - 1, blocks(pl.multiple_of(j * t, t), t, tri_ref, None)

    lax.while_loop(cond, body, (i - 2, top))
    o_ref[...] = acc_ref[...].astype(o_ref.dtype)


def _suffix_ones(width):
    tri = (np.arange(width)[:, None] > np.arange(width)[None, :]).astype(np.float32)
    return jnp.asarray(np.concatenate([tri, tri], axis=0), dtype=BF16)


def _sb_attention(qkv):
    s = qkv.shape[0]
    t = _tile(s, ATTN_TILE)
    assert s >= 2 * t
    w = SB_HEADS_PER_STEP * SB_HEAD_DIM
    groups = SB_HEADS // SB_HEADS_PER_STEP
    return pl.pallas_call(
        _sb_attn_kernel,
        grid=(groups, s // t),
        in_specs=[
            pl.BlockSpec((t, w), lambda g, i: (i, g)),
            pl.BlockSpec((s, w), lambda g, i: (0, groups + g)),
            pl.BlockSpec((s, w), lambda g, i: (0, 2 * groups + g)),
            pl.BlockSpec((4 * t, 2 * t), lambda g, i: (0, 0)),
            pl.BlockSpec((2 * t, t), lambda g, i: (0, 0)),
        ],
        out_specs=pl.BlockSpec((t, w), lambda g, i: (i, g)),
        out_shape=jax.ShapeDtypeStruct((s, SB_W), BF16),
        scratch_shapes=[pltpu.VMEM((t, w), F32), pltpu.VMEM((t, w), F32)],
        compiler_params=_params("parallel", "parallel"),
        name="sb_attention",
    )(qkv, qkv, qkv, _suffix_ones(2 * t), _suffix_ones(t))


def _mla_attn_kernel(fast_ref, q_ref, k_ref, v_ref, o_ref, acc_ref, m_ref):
    tq = q_ref.shape[0]
    t = tq // 2
    i = pl.program_id(1)
    q = q_ref[...]

    def scores(j, chunk_offset):
        start = pl.multiple_of(j * t, t)
        s = lax.dot_general(q, k_ref[pl.ds(start, t), :], (((1,), (1,)), ((), ())), preferred_element_type=F32)
        if chunk_offset is not None:
            row = lax.broadcasted_iota(jnp.int32, (tq, t), 0)
            col = lax.broadcasted_iota(jnp.int32, (tq, t), 1)
            s = jnp.where((col // CHUNK) + chunk_offset <= (row // CHUNK), s, -jnp.inf)
        return s, v_ref[pl.ds(start, t), :]

    def shifted_tile(j, chunk_offset):
        s, v = scores(j, chunk_offset)
        acc_ref[...] += jnp.dot(jnp.exp2(s).astype(BF16), v, preferred_element_type=F32)

    def online_tile(j, chunk_offset):
        s, v = scores(j, chunk_offset)
        m_prev = m_ref[...]
        m_new = jnp.maximum(m_prev, jnp.max(s, axis=-1, keepdims=True))
        alpha = jnp.exp2(m_prev - m_new)
        p = jnp.exp2(s - jnp.concatenate([m_new] * (t // LANES), axis=1))
        acc_ref[...] = jnp.concatenate([alpha, alpha], axis=1) * acc_ref[...] + jnp.dot(
            p.astype(BF16), v, preferred_element_type=F32)
        m_ref[...] = m_new

    def sweep(tile):
        tile(2 * i, 0)
        tile(2 * i + 1, t // CHUNK)

        def pair(p, c):
            tile(2 * p, None)
            tile(2 * p + 1, None)
            return c

        lax.fori_loop(0, i, pair, 0)

    acc_ref[...] = jnp.zeros_like(acc_ref)
    shifted = fast_ref[0] == 1

    @pl.when(shifted)
    def _():
        sweep(shifted_tile)

    @pl.when(jnp.logical_not(shifted))
    def _():
        m_ref[...] = jnp.full_like(m_ref, -jnp.inf)
        sweep(online_tile)

    acc = acc_ref[...]
    o_ref[...] = (acc[:, :V_DIM] / acc[:, V_DIM:]).astype(o_ref.dtype)


def _mla_attention(shifted, q, k, v_ones):
    s = q.shape[0]
    t = _tile(s, 2 * MLA_TILE)
    dk = 2 * LANES
    return pl.pallas_call(
        _mla_attn_kernel,
        grid_spec=pltpu.PrefetchScalarGridSpec(
            num_scalar_prefetch=1,
            grid=(MLA_HEADS, s // t),
            in_specs=[
                pl.BlockSpec((t, dk), lambda h, i, f: (i, h)),
                pl.BlockSpec((s, dk), lambda h, i, f: (0, h)),
                pl.BlockSpec((s, dk), lambda h, i, f: (0, h)),
            ],
            out_specs=pl.BlockSpec((t, V_DIM), lambda h, i, f: (i, h)),
            scratch_shapes=[pltpu.VMEM((t, 2 * V_DIM), F32), pltpu.VMEM((t, LANES), F32)],
        ),
        out_shape=jax.ShapeDtypeStruct((s, MLA_HEADS * V_DIM), BF16),
        compiler_params=_params("parallel", "parallel"),
        name="mla_attention",
    )(shifted, q, k, v_ones)


def _merge_kernel(osb_ref, omla_ref, omem_ref, wsb_ref, wmla_ref, wmem_ref, g0_ref, g1_ref, g2_ref, o_ref):
    acc = g0_ref[...].astype(F32) * jnp.dot(osb_ref[...], wsb_ref[...], preferred_element_type=F32)
    acc += g1_ref[...].astype(F32) * jnp.dot(omla_ref[...], wmla_ref[...], preferred_element_type=F32)
    acc += g2_ref[...].astype(F32) * jnp.dot(omem_ref[...], wmem_ref[...], preferred_element_type=F32)
    o_ref[...] = acc.astype(o_ref.dtype)


def _merge(o_sb, o_mla, o_mem, w_sb_o, w_mla_o, w_mem_o, gates):
    m = o_sb.shape[0]
    d = w_sb_o.shape[1]
    tm = _tile(m, 1024)
    tn = _tile(d, 1024)
    nj = d // tn
    row = lambda i, j: (i, 0)
    colw = lambda i, j: (0, j)
    return pl.pallas_call(
        _merge_kernel,
        grid=(m // tm, nj),
        in_specs=[
            pl.BlockSpec((tm, o_sb.shape[1]), row),
            pl.BlockSpec((tm, o_mla.shape[1]), row),
            pl.BlockSpec((tm, o_mem.shape[1]), row),
            pl.BlockSpec((w_sb_o.shape[0], tn), colw),
            pl.BlockSpec((w_mla_o.shape[0], tn), colw),
            pl.BlockSpec((w_mem_o.shape[0], tn), colw),
            pl.BlockSpec((tm, tn), lambda i, j: (i, j)),
            pl.BlockSpec((tm, tn), lambda i, j: (i, nj + j)),
            pl.BlockSpec((tm, tn), lambda i, j: (i, 2 * nj + j)),
        ],
        out_specs=pl.BlockSpec((tm, tn), lambda i, j: (i, j)),
        out_shape=jax.ShapeDtypeStruct((m, d), BF16),
        compiler_params=_params("parallel", "parallel"),
        name="gated_merge",
    )(o_sb, o_mla, o_mem, w_sb_o, w_mla_o, w_mem_o, gates, gates, gates)


def _mm_kacc_kernel(a_ref, b_ref, r_ref, o_ref):
    @pl.when(pl.program_id(2) == 0)
    def _():
        o_ref[...] = r_ref[...]

    o_ref[...] += jnp.dot(a_ref[...], b_ref[...], preferred_element_type=F32)


def _matmul_kacc_residual(a, b, r, name):
    m, k = a.shape
    n = b.shape[1]
    tm, tn, tk = _tile(m, 1024), _tile(n, 1024), _tile(k, 2048)
    return pl.pallas_call(
        _mm_kacc_kernel,
        grid=(m // tm, n // tn, k // tk),
        in_specs=[
            pl.BlockSpec((tm, tk), lambda i, j, kk: (i, kk)),
            pl.BlockSpec((tk, tn), lambda i, j, kk: (kk, j)),
            pl.BlockSpec((tm, tn), lambda i, j, kk: (i, j)),
        ],
        out_specs=pl.BlockSpec((tm, tn), lambda i, j, kk: (i, j)),
        out_shape=jax.ShapeDtypeStruct((m, n), F32),
        compiler_params=_params("parallel", "parallel", "arbitrary"),
        name=name,
    )(a, b, r)


def _rope_tables():
    half = ROPE_DIM // 2
    freqs = 1.0 / (ROPE_THETA ** (jnp.arange(half, dtype=F32) / half))
    zeros = jnp.zeros((LANES - ROPE_DIM,), F32)
    freq = jnp.concatenate([freqs, freqs, zeros]).reshape(1, LANES)
    cmask = jnp.concatenate([jnp.ones((ROPE_DIM,), F32), zeros]).reshape(1, LANES)
    smask = jnp.concatenate([-jnp.ones((half,), F32), jnp.ones((half,), F32), zeros]).reshape(1, LANES)
    return freq, cmask, smask


def _pad_gain(g):
    return jnp.concatenate([g, jnp.zeros((LANES - g.shape[0],), g.dtype)]).reshape(1, LANES)


def _layer(x, mem, positions, g_mix, g_mem, w_in, g_cq, g_ckv, w_q_b, w_kv_b, g_q_mla, g_k_mla, w_mem_kv,
           g_q_mem, g_k_mem, w_sb_o, w_mla_o, w_mem_o, w_out, g_ffn, w_ff1, w_ff2):
    s, d = x.shape
    pos = positions.reshape(s, 1)

    c0 = 3 * SB_W
    c1 = c0 + Q_LORA
    c2 = c1 + KV_LORA
    c3 = c2 + ROPE_DIM
    c4 = c3 + MEM_W
    w_t = _cast_bf16(jnp.transpose(w_in), "cast_w_in")
    half = ROPE_DIM // 2
    wq = w_q_b.reshape(Q_LORA, MLA_HEADS, QK_DIM)
    wq_pad = jnp.zeros((Q_LORA, MLA_HEADS, LANES - ROPE_DIM), wq.dtype)
    w_qb = jnp.concatenate([wq, wq_pad, wq[..., NOPE_DIM + half:], wq[..., NOPE_DIM:NOPE_DIM + half], wq_pad], axis=-1)
    w_qb = w_qb.reshape(Q_LORA, MLA_HEADS * 3 * LANES).astype(BF16)
    w_kvb = w_kv_b.astype(BF16)

    bound = np.float32(np.sqrt(QK_DIM) * np.log2(np.e)) * jnp.max(jnp.abs(g_q_mla)) * jnp.max(jnp.abs(g_k_mla))
    shifted = bound <= MLA_SHIFT_BOUND
    shift_lane = (jnp.arange(LANES) == ROPE_DIM).astype(F32).reshape(1, LANES)
    q_bias = shift_lane * jnp.where(shifted, -bound, 0.0)
    k_bias = shift_lane

    h = _rmsnorm(x, g_mix, "rmsnorm_mix")

    tm = _tile(s, 1024)

    sb_scale = -1.0 / np.sqrt(SB_HEAD_DIM).astype(np.float32)
    colscale = jnp.concatenate([jnp.full((SB_W,), sb_scale, F32), jnp.ones((2 * SB_W,), F32)]).reshape(1, c0)
    tn = _tile(c0, 768)
    (sb_qkv,) = _matmul(
        h, w_t, b_transposed=True, n=c0, tm=tm, tn=tn,
        extra=[(colscale, (1, tn), lambda i, j: (0, j))],
        outs=[((s, c0), BF16, (tm, tn), lambda i, j: (i, j))],
        epilogue=_ep_colscale, name="proj_sb_qkv")
    o_sb = _sb_attention(sb_qkv)

    freq, cmask, smask = _rope_tables()
    tml = _tile(s, 512)
    nlat = c3 + LANES - ROPE_DIM - c0
    full = lambda shape: (shape, lambda i, j: (0, 0))
    rows = lambda w: ((tml, w), lambda i, j: (i, 0))
    side = [w_out] if _slab_ok(w_out, s // tml) else []
    cq_n, ckv_n, krot, cos_t, sin_t, *casted = _matmul(
        h, w_t, b_transposed=True, b_col0=c0, n=nlat, tm=tml, tn=nlat, cast=side,
        extra=[(pos, *rows(1)), (freq, *full((1, LANES))), (cmask, *full((1, LANES))), (smask, *full((1, LANES))),
               (g_cq.reshape(1, Q_LORA), *full((1, Q_LORA))), (g_ckv.reshape(1, KV_LORA), *full((1, KV_LORA)))],
        outs=[((s, Q_LORA), BF16, *rows(Q_LORA)), ((s, KV_LORA), BF16, *rows(KV_LORA)),
              ((s, LANES), F32, *rows(LANES)), ((s, LANES), F32, *rows(LANES)), ((s, LANES), F32, *rows(LANES))],
        epilogue=_ep_latent, name="proj_mla_latent")
    w_out_b = casted[0] if side else w_out.astype(BF16)

    gq_n, gq_r = g_q_mla[:NOPE_DIM].reshape(1, LANES), _pad_gain(g_q_mla[NOPE_DIM:])
    gk_n, gk_r = g_k_mla[:NOPE_DIM].reshape(1, LANES), _pad_gain(g_k_mla[NOPE_DIM:])
    rows = lambda w: ((tm, w), lambda i, j: (i, 0))
    mla_scale = np.float32(np.log2(np.e) / np.sqrt(QK_DIM))
    tnh = MLA_HEADS_PER_STEP * 2 * LANES
    (q_mla,) = _matmul(
        cq_n, w_qb, tm=tm, tn=MLA_HEADS_PER_STEP * 3 * LANES,
        extra=[(cos_t, *rows(LANES)), (sin_t, *rows(LANES)), (gq_n, *full((1, LANES))), (gq_r, *full((1, LANES))),
               (q_bias, *full((1, LANES)))],
        outs=[((s, MLA_HEADS * 2 * LANES), BF16, (tm, tnh), lambda i, j: (i, j))],
        epilogue=functools.partial(_ep_mla_q, mla_scale), name="mla_q_up")
    k_mla, v_mla = _matmul(
        ckv_n, w_kvb, tm=tm, tn=tnh,
        extra=[(krot, *rows(LANES)), (gk_n, *full((1, LANES))), (gk_r, *full((1, LANES))),
               (k_bias, *full((1, LANES)))],
        outs=[((s, MLA_HEADS * 2 * LANES), BF16, (tm, tnh), lambda i, j: (i, j)),
              ((s, MLA_HEADS * 2 * LANES), BF16, (tm, tnh), lambda i, j: (i, j))],
        epilogue=_ep_mla_kv, name="mla_kv_up")
    o_mla = _mla_attention(shifted.astype(jnp.int32).reshape(1), q_mla, k_mla, v_mla)

    mkv = _memkv(mem, g_mem, w_mem_kv.astype(BF16), g_k_mem)
    n_mem = mem.shape[0]
    tmm = _tile(s, 512)
    branch_w = [w_sb_o, w_mla_o, w_mem_o]
    side = branch_w if all(_slab_ok(w, s // tmm) for w in branch_w) else []
    o_mem, *casted = _matmul(
        h, w_t, b_transposed=True, b_col0=c3, n=MEM_W, tm=tmm, tn=MEM_W, cast=side,
        extra=[(mkv, (n_mem, MEM_W), lambda i, j: (0, 0)), (mkv, (n_mem, MEM_W), lambda i, j: (0, 1)),
               (g_q_mem.reshape(1, MEM_HEAD_DIM), (1, MEM_HEAD_DIM), lambda i, j: (0, 0))],
        outs=[((s, MEM_W), BF16, (tmm, MEM_W), lambda i, j: (i, 0))],
        epilogue=_ep_mem_attn, name="mem_attention")
    w_sb_o_b, w_mla_o_b, w_mem_o_b = casted if side else [w.astype(BF16) for w in branch_w]

    tng = _tile(3 * d, 1024)
    (gates,) = _matmul(
        h, w_t, b_transposed=True, b_col0=c4, n=3 * d, tm=tm, tn=tng,
        outs=[((s, 3 * d), BF16, (tm, tng), lambda i, j: (i, j))],
        epilogue=_ep_sigmoid, name="proj_gates")
    merged = _merge(o_sb, o_mla, o_mem, w_sb_o_b, w_mla_o_b, w_mem_o_b, gates)
    tnd = _tile(d, 512)
    side = [w_ff1] if _slab_ok(w_ff1, (s // tm) * (d // tnd)) else []
    x1, *casted = _matmul(
        merged, w_out_b, tm=tm, tn=tnd, cast=side,
        extra=[(x, (tm, tnd), lambda i, j: (i, j))],
        outs=[((s, d), F32, (tm, tnd), lambda i, j: (i, j))],
        epilogue=_ep_residual, name="out_proj")
    w_ff1_b = casted[0] if side else w_ff1.astype(BF16)

    hn = _rmsnorm(x1, g_ffn, "rmsnorm_ffn")
    dff = w_ff1.shape[1]
    tnf = _tile(dff, 1024)
    side = [w_ff2] if _slab_ok(w_ff2, (s // tm) * (dff // tnf)) else []
    u2, *casted = _matmul(
        hn, w_ff1_b, tm=tm, tn=tnf, cast=side,
        outs=[((s, dff), BF16, (tm, tnf), lambda i, j: (i, j))],
        epilogue=_ep_relu2, name="ffn_up")
    w_ff2_b = casted[0] if side else w_ff2.astype(BF16)
    return _matmul_kacc_residual(u2, w_ff2_b, x1, "ffn_down")


def kernel(x, mem, positions, g_mix, g_mem, w_in, g_cq, g_ckv, w_q_b, w_kv_b, g_q_mla, g_k_mla, w_mem_kv, g_q_mem,
           g_k_mem, w_sb_o, w_mla_o, w_mem_o, w_out, g_ffn, w_ff1, w_ff2):
    depth = w_in.shape[0]
    assert x.shape[0] == 1 and mem.shape[0] == 1
    y = x[0]
    for i in range(depth):
        y = _layer(y, mem[0], positions[0], g_mix[i], g_mem[i], w_in[i], g_cq[i], g_ckv[i], w_q_b[i], w_kv_b[i],
                   g_q_mla[i], g_k_mla[i], w_mem_kv[i], g_q_mem[i], g_k_mem[i], w_sb_o[i], w_mla_o[i], w_mem_o[i],
                   w_out[i], g_ffn[i], w_ff1[i], w_ff2[i])
    return y[None]
```

```python
import functools
from typing import Any, NamedTuple

import jax
import jax.numpy as jnp
import numpy as np
from jax import lax
from jax.experimental import pallas as pl
from jax.experimental.pallas import tpu as pltpu

F32 = jnp.float32
BF16 = jnp.bfloat16

CHUNK = 64
SB_HEADS = 12
SB_HEAD_DIM = 128
SB_W = SB_HEADS * SB_HEAD_DIM
MLA_HEADS = 12
Q_LORA = 896
KV_LORA = 512
NOPE_DIM = 128
ROPE_DIM = 64
V_DIM = 128
QK_DIM = NOPE_DIM + ROPE_DIM
MEM_HEADS = 4
MEM_HEAD_DIM = 256
MEM_W = MEM_HEADS * MEM_HEAD_DIM
ROPE_THETA = 10000.0
EPS = 1e-6

LANES = 128
BF16_SUBLANES = 16
F32_SUBLANES = 8
SIDE_CAST_MAX_BYTES = 4 * 1024 * 1024
VMEM_LIMIT_BYTES = 60 * 1024 * 1024
SB_DEAD_LOG = -120.0
ATTN_TILE = 256
SB_HEADS_PER_STEP = 4
MLA_TILE = 512
MLA_SHIFT_BOUND = 50.0
MLA_HEADS_PER_STEP = 4


def _params(*sem):
    return pltpu.CompilerParams(dimension_semantics=sem, vmem_limit_bytes=VMEM_LIMIT_BYTES)


def _tile(n, pref):
    t = int(min(n, pref))
    assert n % t == 0, (n, t)
    return t


def _rmsnorm_kernel(x_ref, g_ref, o_ref):
    x = x_ref[...]
    ms = jnp.mean(x * x, axis=-1, keepdims=True)
    o_ref[...] = (x * lax.rsqrt(ms + EPS) * g_ref[...]).astype(o_ref.dtype)


def _rmsnorm(x, g, name):
    m, d = x.shape
    tm = _tile(m, 256)
    return pl.pallas_call(
        _rmsnorm_kernel,
        grid=(m // tm,),
        in_specs=[pl.BlockSpec((tm, d), lambda i: (i, 0)), pl.BlockSpec((1, d), lambda i: (0, 0))],
        out_specs=pl.BlockSpec((tm, d), lambda i: (i, 0)),
        out_shape=jax.ShapeDtypeStruct((m, d), BF16),
        compiler_params=_params("parallel"),
        name=name,
    )(x, g.reshape(1, d))


def _cast_kernel(w_ref, o_ref):
    o_ref[...] = w_ref[...].astype(o_ref.dtype)


def _cast_bf16(w, name, rows=None):
    cols = w.shape[1]
    rows = w.shape[0] if rows is None else rows
    assert rows % BF16_SUBLANES == 0
    units = rows // BF16_SUBLANES
    tr = BF16_SUBLANES * max(u for u in range(1, 129) if units % u == 0)
    tc = _tile(cols, 1024)
    spec = pl.BlockSpec((tr, tc), lambda i, j: (i, j))
    return pl.pallas_call(
        _cast_kernel,
        grid=(rows // tr, cols // tc),
        in_specs=[spec],
        out_specs=spec,
        out_shape=jax.ShapeDtypeStruct((rows, cols), BF16),
        compiler_params=_params("parallel", "parallel"),
        name=name,
    )(w)


class _SideCast(NamedTuple):
    w: jax.Array
    row0: int
    nrows: int
    slab: int
    scale: Any = None


def _side_cast(w, steps, row0=0, nrows=None, scale=None):
    nrows = w.shape[0] - row0 if nrows is None else nrows
    if row0 % F32_SUBLANES:
        return None
    for slab in range(BF16_SUBLANES, nrows + 1, BF16_SUBLANES):
        if nrows % slab == 0 and nrows // slab <= steps:
            fits = slab * w.shape[1] * 4 <= SIDE_CAST_MAX_BYTES
            return _SideCast(w, row0, nrows, slab, scale) if fits else None
    return None


def _mm_kernel(epilogue, n_extra, cast_has_scale, b_transposed, a_ref, b_ref, *refs):
    n_cast = len(cast_has_scale)
    n_scale = sum(cast_has_scale)
    n_in = n_extra + n_cast + n_scale
    extra, cast_in, scales = refs[:n_extra], refs[n_extra:n_extra + n_cast], list(refs[n_extra + n_cast:n_in])
    outs, cast_out = refs[n_in:len(refs) - n_cast], refs[len(refs) - n_cast:]
    if b_transposed:
        acc = lax.dot_general(a_ref[...], b_ref[...], (((1,), (1,)), ((), ())), preferred_element_type=F32)
    else:
        acc = jnp.dot(a_ref[...], b_ref[...], preferred_element_type=F32)
    epilogue(acc, extra, outs)
    for src_ref, dst_ref, has_scale in zip(cast_in, cast_out, cast_has_scale):
        w = src_ref[...]
        if has_scale:
            w = w * scales.pop(0)[...]
        dst_ref[...] = w.astype(dst_ref.dtype)


def _matmul(a, b, *, tm, tn, extra=(), outs, epilogue, name, b_col0=0, n=None, cast=(), b_transposed=False,
            semantics=("parallel", "parallel")):
    m, k = a.shape
    k2, n_total = (b.shape[1], b.shape[0]) if b_transposed else b.shape
    n = n_total if n is None else n
    assert k == k2 and m % tm == 0 and n % tn == 0 and b_col0 + n <= n_total
    gm, gn = m // tm, n // tn
    if b_transposed:
        assert b_col0 % BF16_SUBLANES == 0 and tn % BF16_SUBLANES == 0
        b_spec = pl.BlockSpec((pl.Element(tn), pl.Element(k)),
                              lambda i, j: (pl.multiple_of(b_col0 + j * tn, BF16_SUBLANES), 0))
    else:
        assert b_col0 % tn == 0
        jb = b_col0 // tn
        b_spec = pl.BlockSpec((k, tn), lambda i, j: (0, jb + j))
    in_specs = [pl.BlockSpec((tm, k), lambda i, j: (i, 0)), b_spec]
    in_specs += [pl.BlockSpec(bs, im) for _, bs, im in extra]
    out_specs = [pl.BlockSpec(bs, im) for _, _, bs, im in outs]
    out_shape = [jax.ShapeDtypeStruct(s, dt) for s, dt, _, _ in outs]
    scale_specs, scale_args = [], []
    for c in cast:
        n_slabs = c.nrows // c.slab
        assert n_slabs * c.slab == c.nrows and n_slabs <= gm * gn, (name, c.nrows, c.slab, gm, gn)
        cols = c.w.shape[1]

        def slab_index(i, j, n_slabs=n_slabs):
            return jnp.minimum(i * gn + j, n_slabs - 1)

        def src_spec(width, c=c, slab_index=slab_index):
            if c.row0 % c.slab == 0:
                first = c.row0 // c.slab
                return pl.BlockSpec((c.slab, width), lambda i, j: (first + slab_index(i, j), 0))
            return pl.BlockSpec(
                (pl.Element(c.slab), pl.Element(width)),
                lambda i, j: (pl.multiple_of(c.row0 + slab_index(i, j) * c.slab, F32_SUBLANES), 0))

        in_specs.append(src_spec(cols))
        if c.scale is not None:
            scale_specs.append(src_spec(1))
            scale_args.append(c.scale)
        out_specs.append(pl.BlockSpec((c.slab, cols), lambda i, j, slab_index=slab_index: (slab_index(i, j), 0)))
        out_shape.append(jax.ShapeDtypeStruct((c.nrows, cols), BF16))
    res = pl.pallas_call(
        functools.partial(_mm_kernel, epilogue, len(extra), tuple(c.scale is not None for c in cast), b_transposed),
        grid=(gm, gn),
        in_specs=in_specs + scale_specs,
        out_specs=out_specs,
        out_shape=out_shape,
        compiler_params=_params(*semantics),
        name=name,
    )(a, b, *[x for x, _, _ in extra], *[c.w for c in cast], *scale_args)
    return res


def _sumsq(x):
    return jnp.sum(x * x, axis=-1, keepdims=True)


def _sumsq2(x, y):
    return jnp.sum(x * x + y * y, axis=-1, keepdims=True)


def _swap_rope_halves(x):
    half = ROPE_DIM // 2
    lane = lax.broadcasted_iota(jnp.int32, x.shape, 1)
    return jnp.where(lane < half, pltpu.roll(x, LANES - half, 1), pltpu.roll(x, half, 1))


def _ep_colscale(acc, extra, outs):
    (scale_ref,) = extra
    (o_ref,) = outs
    o_ref[...] = (acc * scale_ref[...]).astype(o_ref.dtype)


def _ep_sigmoid(acc, extra, outs):
    (o_ref,) = outs
    o_ref[...] = (0.5 * jnp.tanh(0.5 * acc) + 0.5).astype(o_ref.dtype)


def _ep_latent(acc, extra, outs):
    pos_ref, freq_ref, cmask_ref, smask_ref, gq_ref, gkv_ref = extra
    cq_ref, ckv_ref, krot_ref, cos_ref, sin_ref = outs
    cq = acc[:, :Q_LORA]
    inv = lax.rsqrt(_sumsq(cq) * (1.0 / Q_LORA) + EPS)
    cq_ref[...] = (cq * inv * gq_ref[...]).astype(cq_ref.dtype)
    ckv = acc[:, Q_LORA:Q_LORA + KV_LORA]
    inv = lax.rsqrt(_sumsq(ckv) * (1.0 / KV_LORA) + EPS)
    ckv_ref[...] = (ckv * inv * gkv_ref[...]).astype(ckv_ref.dtype)
    ang = pos_ref[...].astype(F32) * freq_ref[...]
    cos_t = jnp.cos(ang) * cmask_ref[...]
    sin_t = jnp.sin(ang) * smask_ref[...]
    base = Q_LORA + KV_LORA
    ka = acc[:, base:base + LANES]
    kb = _swap_rope_halves(ka)
    krot_ref[...] = ka * cos_t + kb * sin_t
    cos_ref[...] = cos_t
    sin_ref[...] = sin_t


def _ep_mla_q(scale, acc, extra, outs):
    cos_ref, sin_ref, gn_ref, gr_ref, bias_ref = extra
    (o_ref,) = outs
    for hh in range(acc.shape[1] // (3 * LANES)):
        src, dst = hh * 3 * LANES, hh * 2 * LANES
        nope = acc[:, src:src + LANES]
        rot = acc[:, src + LANES:src + 2 * LANES] * cos_ref[...] + acc[:, src + 2 * LANES:src + 3 * LANES] * sin_ref[...]
        inv = lax.rsqrt(_sumsq2(nope, rot) * (1.0 / QK_DIM) + EPS) * scale
        o_ref[:, dst:dst + LANES] = (nope * inv * gn_ref[...]).astype(o_ref.dtype)
        o_ref[:, dst + LANES:dst + 2 * LANES] = (rot * inv * gr_ref[...] + bias_ref[...]).astype(o_ref.dtype)


def _ep_mla_kv(acc, extra, outs):
    krot_ref, gn_ref, gr_ref, bias_ref = extra
    k_ref, v_ref = outs
    rot = krot_ref[...]
    for base in range(0, acc.shape[1], 2 * LANES):
        nope = acc[:, base:base + LANES]
        inv = lax.rsqrt(_sumsq2(nope, rot) * (1.0 / QK_DIM) + EPS)
        k_ref[:, base:base + LANES] = (nope * inv * gn_ref[...]).astype(k_ref.dtype)
        k_ref[:, base + LANES:base + 2 * LANES] = (rot * inv * gr_ref[...] + bias_ref[...]).astype(k_ref.dtype)
        v_ref[:, base:base + LANES] = acc[:, base + LANES:base + 2 * LANES].astype(v_ref.dtype)
        v_ref[:, base + LANES:base + 2 * LANES] = jnp.ones((acc.shape[0], LANES), v_ref.dtype)


def _ep_mem_attn(acc, extra, outs):
    mk_ref, mv_ref, gq_ref = extra
    (o_ref,) = outs
    scale = 1.0 / np.sqrt(MEM_HEAD_DIM)
    for h in range(MEM_HEADS):
        sl = slice(h * MEM_HEAD_DIM, (h + 1) * MEM_HEAD_DIM)
        qh = acc[:, sl]
        inv = lax.rsqrt(_sumsq(qh) * (1.0 / MEM_HEAD_DIM) + EPS) * scale
        qn = (qh * inv * gq_ref[...]).astype(BF16)
        s = lax.dot_general(qn, mk_ref[:, sl], (((1,), (1,)), ((), ())), preferred_element_type=F32)
        p = jnp.exp(s - jnp.max(s, axis=-1, keepdims=True))
        den = jnp.sum(p, axis=-1, keepdims=True)
        o = jnp.dot(p.astype(BF16), mv_ref[:, sl], preferred_element_type=F32)
        o_ref[:, sl] = (o / den).astype(o_ref.dtype)


def _ep_relu2_rms(d_model, acc, extra, outs):
    (ss_ref,) = extra
    (o_ref,) = outs
    inv = lax.rsqrt(ss_ref[...] * (1.0 / d_model) + EPS)
    u = jnp.maximum(acc * jnp.concatenate([inv] * (acc.shape[1] // LANES), axis=1), 0.0)
    o_ref[...] = (u * u).astype(o_ref.dtype)


def _ep_residual_stats(acc, extra, outs):
    (r_ref,) = extra
    x1_ref, xb_ref, ss_ref = outs
    x1 = r_ref[...] + acc
    x1_ref[...] = x1
    xb_ref[...] = x1.astype(xb_ref.dtype)
    part = jnp.broadcast_to(_sumsq(x1), ss_ref.shape)

    @pl.when(pl.program_id(1) == 0)
    def _():
        ss_ref[...] = part

    @pl.when(pl.program_id(1) != 0)
    def _():
        ss_ref[...] += part


def _memkv_kernel(mem_ref, g_ref, w_ref, gk_ref, o_ref):
    x = mem_ref[...]
    ms = jnp.mean(x * x, axis=-1, keepdims=True)
    hn = (x * lax.rsqrt(ms + EPS) * g_ref[...]).astype(BF16)
    acc = jnp.dot(hn, w_ref[...], preferred_element_type=F32)
    j = pl.program_id(0)
    inv = lax.rsqrt(_sumsq(acc) * (1.0 / MEM_HEAD_DIM) + EPS)
    normed = acc * inv * gk_ref[...]
    o_ref[...] = jnp.where(j < MEM_HEADS, normed, acc).astype(o_ref.dtype)


def _memkv(mem, g_mem, w_mem_kv_bf16, g_k_mem):
    n_mem, d = mem.shape
    n = w_mem_kv_bf16.shape[1]
    return pl.pallas_call(
        _memkv_kernel,
        grid=(n // MEM_HEAD_DIM,),
        in_specs=[
            pl.BlockSpec((n_mem, d), lambda j: (0, 0)),
            pl.BlockSpec((1, d), lambda j: (0, 0)),
            pl.BlockSpec((d, MEM_HEAD_DIM), lambda j: (0, j)),
            pl.BlockSpec((1, MEM_HEAD_DIM), lambda j: (0, 0)),
        ],
        out_specs=pl.BlockSpec((n_mem, MEM_HEAD_DIM), lambda j: (0, j)),
        out_shape=jax.ShapeDtypeStruct((n_mem, n), BF16),
        compiler_params=_params("parallel"),
        name="mem_kv",
    )(mem, g_mem.reshape(1, d), w_mem_kv_bf16, g_k_mem.reshape(1, MEM_HEAD_DIM))


def _sb_attn_kernel(q_ref, k_ref, v_ref, tri_win_ref, tri_ref, o_ref, acc_ref, carry_ref):
    t = q_ref.shape[0]
    hd = SB_HEAD_DIM
    i = pl.program_id(1)

    def block(head, start, width, tri_ref, mask_offset):
        cols = slice(head * hd, (head + 1) * hd)
        k = k_ref[pl.ds(start, width), cols]
        v = v_ref[pl.ds(start, width), cols]
        nz = lax.dot_general(q_ref[:, cols], k, (((1,), (1,)), ((), ())), preferred_element_type=F32)
        l = jnp.minimum(nz, 0.0) - jnp.log(1.0 + jnp.exp(-jnp.abs(nz)))
        if mask_offset is not None:
            row = lax.broadcasted_iota(jnp.int32, (t, width), 0)
            col = lax.broadcasted_iota(jnp.int32, (t, width), 1)
            causal = col < row + mask_offset
            l = jnp.where(causal, l, 0.0)
        l_hi = l.astype(BF16)
        l_lo = (l - l_hi.astype(F32)).astype(BF16)
        suffix = jnp.dot(jnp.concatenate([l_hi, l_lo], axis=1), tri_ref[...], preferred_element_type=F32)
        carry = carry_ref[:, cols]
        a = jnp.exp((l - nz) + suffix + jnp.concatenate([carry] * (width // LANES), axis=1))
        if mask_offset is not None:
            a = jnp.where(causal, a, 0.0)
        acc_ref[:, cols] += jnp.dot(a.astype(BF16), v, preferred_element_type=F32)
        new_carry = carry + jnp.broadcast_to(suffix[:, 0:1] + l[:, 0:1], carry.shape)
        carry_ref[:, cols] = new_carry
        return jnp.max(new_carry)

    def blocks(start, width, tri_ref, mask_offset):
        tops = [block(head, start, width, tri_ref, mask_offset) for head in range(SB_HEADS_PER_STEP)]
        return functools.reduce(jnp.maximum, tops)

    acc_ref[...] = jnp.zeros_like(acc_ref)
    carry_ref[...] = jnp.zeros_like(carry_ref)
    first = jnp.maximum(i - 1, 0)
    top = blocks(pl.multiple_of(first * t, t), 2 * t, tri_win_ref, (i - first) * t)

    def cond(state):
        j, top = state
        return jnp.logical_and(j >= 0, top > SB_DEAD_LOG)

    def body(state):
        j, _ = state
        return j - 1, blocks(pl.multiple_of(j * t, t), t, tri_ref, None)

    lax.while_loop(cond, body, (i - 2, top))
    o_ref[...] = acc_ref[...].astype(o_ref.dtype)


def _suffix_ones(width):
    tri = (np.arange(width)[:, None] > np.arange(width)[None, :]).astype(np.float32)
    return jnp.asarray(np.concatenate([tri, tri], axis=0), dtype=BF16)


def _sb_attention(qkv):
    s = qkv.shape[0]
    t = _tile(s, ATTN_TILE)
    assert s >= 2 * t
    w = SB_HEADS_PER_STEP * SB_HEAD_DIM
    groups = SB_HEADS // SB_HEADS_PER_STEP
    return pl.pallas_call(
        _sb_attn_kernel,
        grid=(groups, s // t),
        in_specs=[
            pl.BlockSpec((t, w), lambda g, i: (i, g)),
            pl.BlockSpec((s, w), lambda g, i: (0, groups + g)),
            pl.BlockSpec((s, w), lambda g, i: (0, 2 * groups + g)),
            pl.BlockSpec((4 * t, 2 * t), lambda g, i: (0, 0)),
            pl.BlockSpec((2 * t, t), lambda g, i: (0, 0)),
        ],
        out_specs=pl.BlockSpec((t, w), lambda g, i: (i, g)),
        out_shape=jax.ShapeDtypeStruct((s, SB_W), BF16),
        scratch_shapes=[pltpu.VMEM((t, w), F32), pltpu.VMEM((t, w), F32)],
        compiler_params=_params("parallel", "parallel"),
        name="sb_attention",
    )(qkv, qkv, qkv, _suffix_ones(2 * t), _suffix_ones(t))


def _mla_attn_kernel(fast_ref, q_ref, k_ref, v_ref, o_ref, acc_ref, m_ref):
    tq = q_ref.shape[0]
    t = tq // 2
    i = pl.program_id(1)
    q = q_ref[...]

    def scores(j, chunk_offset):
        start = pl.multiple_of(j * t, t)
        s = lax.dot_general(q, k_ref[pl.ds(start, t), :], (((1,), (1,)), ((), ())), preferred_element_type=F32)
        if chunk_offset is not None:
            row = lax.broadcasted_iota(jnp.int32, (tq, t), 0)
            col = lax.broadcasted_iota(jnp.int32, (tq, t), 1)
            s = jnp.where((col // CHUNK) + chunk_offset <= (row // CHUNK), s, -jnp.inf)
        return s, v_ref[pl.ds(start, t), :]

    def shifted_tile(j, chunk_offset):
        s, v = scores(j, chunk_offset)
        acc_ref[...] += jnp.dot(jnp.exp2(s).astype(BF16), v, preferred_element_type=F32)

    def online_tile(j, chunk_offset):
        s, v = scores(j, chunk_offset)
        m_prev = m_ref[...]
        m_new = jnp.maximum(m_prev, jnp.max(s, axis=-1, keepdims=True))
        alpha = jnp.exp2(m_prev - m_new)
        p = jnp.exp2(s - jnp.concatenate([m_new] * (t // LANES), axis=1))
        acc_ref[...] = jnp.concatenate([alpha, alpha], axis=1) * acc_ref[...] + jnp.dot(
            p.astype(BF16), v, preferred_element_type=F32)
        m_ref[...] = m_new

    def sweep(tile):
        tile(2 * i, 0)
        tile(2 * i + 1, t // CHUNK)

        def pair(p, c):
            tile(2 * p, None)
            tile(2 * p + 1, None)
            return c

        lax.fori_loop(0, i, pair, 0)

    acc_ref[...] = jnp.zeros_like(acc_ref)
    shifted = fast_ref[0] == 1

    @pl.when(shifted)
    def _():
        sweep(shifted_tile)

    @pl.when(jnp.logical_not(shifted))
    def _():
        m_ref[...] = jnp.full_like(m_ref, -jnp.inf)
        sweep(online_tile)

    acc = acc_ref[...]
    o_ref[...] = (acc[:, :V_DIM] / acc[:, V_DIM:]).astype(o_ref.dtype)


def _mla_attention(shifted, q, k, v_ones):
    s = q.shape[0]
    t = _tile(s, 2 * MLA_TILE)
    dk = 2 * LANES
    return pl.pallas_call(
        _mla_attn_kernel,
        grid_spec=pltpu.PrefetchScalarGridSpec(
            num_scalar_prefetch=1,
            grid=(MLA_HEADS, s // t),
            in_specs=[
                pl.BlockSpec((t, dk), lambda h, i, f: (i, h)),
                pl.BlockSpec((s, dk), lambda h, i, f: (0, h)),
                pl.BlockSpec((s, dk), lambda h, i, f: (0, h)),
            ],
            out_specs=pl.BlockSpec((t, V_DIM), lambda h, i, f: (i, h)),
            scratch_shapes=[pltpu.VMEM((t, 2 * V_DIM), F32), pltpu.VMEM((t, LANES), F32)],
        ),
        out_shape=jax.ShapeDtypeStruct((s, MLA_HEADS * V_DIM), BF16),
        compiler_params=_params("parallel", "parallel"),
        name="mla_attention",
    )(shifted, q, k, v_ones)


def _merge_kernel(osb_ref, omla_ref, omem_ref, wsb_ref, wmla_ref, wmem_ref, g0_ref, g1_ref, g2_ref, o_ref):
    acc = g0_ref[...].astype(F32) * jnp.dot(osb_ref[...], wsb_ref[...], preferred_element_type=F32)
    acc += g1_ref[...].astype(F32) * jnp.dot(omla_ref[...], wmla_ref[...], preferred_element_type=F32)
    acc += g2_ref[...].astype(F32) * jnp.dot(omem_ref[...], wmem_ref[...], preferred_element_type=F32)
    o_ref[...] = acc.astype(o_ref.dtype)


def _merge(o_sb, o_mla, o_mem, w_sb_o, w_mla_o, w_mem_o, gates):
    m = o_sb.shape[0]
    d = w_sb_o.shape[1]
    tm = _tile(m, 1024)
    tn = _tile(d, 1024)
    nj = d // tn
    row = lambda i, j: (i, 0)
    colw = lambda i, j: (0, j)
    return pl.pallas_call(
        _merge_kernel,
        grid=(m // tm, nj),
        in_specs=[
            pl.BlockSpec((tm, o_sb.shape[1]), row),
            pl.BlockSpec((tm, o_mla.shape[1]), row),
            pl.BlockSpec((tm, o_mem.shape[1]), row),
            pl.BlockSpec((w_sb_o.shape[0], tn), colw),
            pl.BlockSpec((w_mla_o.shape[0], tn), colw),
            pl.BlockSpec((w_mem_o.shape[0], tn), colw),
            pl.BlockSpec((tm, tn), lambda i, j: (i, j)),
            pl.BlockSpec((tm, tn), lambda i, j: (i, nj + j)),
            pl.BlockSpec((tm, tn), lambda i, j: (i, 2 * nj + j)),
        ],
        out_specs=pl.BlockSpec((tm, tn), lambda i, j: (i, j)),
        out_shape=jax.ShapeDtypeStruct((m, d), BF16),
        compiler_params=_params("parallel", "parallel"),
        name="gated_merge",
    )(o_sb, o_mla, o_mem, w_sb_o, w_mla_o, w_mem_o, gates, gates, gates)


def _mm_kacc_kernel(a_ref, b_ref, r_ref, o_ref):
    @pl.when(pl.program_id(2) == 0)
    def _():
        o_ref[...] = r_ref[...]

    o_ref[...] += jnp.dot(a_ref[...], b_ref[...], preferred_element_type=F32)


def _matmul_kacc_residual(a, b, r, name):
    m, k = a.shape
    n = b.shape[1]
    tm, tn, tk = _tile(m, 1024), _tile(n, 1024), _tile(k, 4096)
    return pl.pallas_call(
        _mm_kacc_kernel,
        grid=(m // tm, n // tn, k // tk),
        in_specs=[
            pl.BlockSpec((tm, tk), lambda i, j, kk: (i, kk)),
            pl.BlockSpec((tk, tn), lambda i, j, kk: (kk, j)),
            pl.BlockSpec((tm, tn), lambda i, j, kk: (i, j)),
        ],
        out_specs=pl.BlockSpec((tm, tn), lambda i, j, kk: (i, j)),
        out_shape=jax.ShapeDtypeStruct((m, n), F32),
        compiler_params=_params("parallel", "parallel", "arbitrary"),
        name=name,
    )(a, b, r)


def _rope_tables():
    half = ROPE_DIM // 2
    freqs = 1.0 / (ROPE_THETA ** (jnp.arange(half, dtype=F32) / half))
    zeros = jnp.zeros((LANES - ROPE_DIM,), F32)
    freq = jnp.concatenate([freqs, freqs, zeros]).reshape(1, LANES)
    cmask = jnp.concatenate([jnp.ones((ROPE_DIM,), F32), zeros]).reshape(1, LANES)
    smask = jnp.concatenate([-jnp.ones((half,), F32), jnp.ones((half,), F32), zeros]).reshape(1, LANES)
    return freq, cmask, smask


def _pad_gain(g):
    return jnp.concatenate([g, jnp.zeros((LANES - g.shape[0],), g.dtype)]).reshape(1, LANES)


def _layer(x, mem, positions, g_mix, g_mem, w_in, g_cq, g_ckv, w_q_b, w_kv_b, g_q_mla, g_k_mla, w_mem_kv,
           g_q_mem, g_k_mem, w_sb_o, w_mla_o, w_mem_o, w_out, g_ffn, w_ff1, w_ff2):
    s, d = x.shape
    pos = positions.reshape(s, 1)

    c0 = 3 * SB_W
    c1 = c0 + Q_LORA
    c2 = c1 + KV_LORA
    c3 = c2 + ROPE_DIM
    c4 = c3 + MEM_W
    w_in_t = jnp.transpose(w_in)
    w_t = _cast_bf16(w_in_t, "cast_w_in", rows=c4)
    half = ROPE_DIM // 2
    wq = w_q_b.reshape(Q_LORA, MLA_HEADS, QK_DIM)
    wq_pad = jnp.zeros((Q_LORA, MLA_HEADS, LANES - ROPE_DIM), wq.dtype)
    w_qb = jnp.concatenate([wq, wq_pad, wq[..., NOPE_DIM + half:], wq[..., NOPE_DIM:NOPE_DIM + half], wq_pad], axis=-1)
    w_qb = w_qb.reshape(Q_LORA, MLA_HEADS * 3 * LANES).astype(BF16)
    w_kvb = w_kv_b.astype(BF16)

    bound = np.float32(np.sqrt(QK_DIM) * np.log2(np.e)) * jnp.max(jnp.abs(g_q_mla)) * jnp.max(jnp.abs(g_k_mla))
    shifted = bound <= MLA_SHIFT_BOUND
    shift_lane = (jnp.arange(LANES) == ROPE_DIM).astype(F32).reshape(1, LANES)
    q_bias = shift_lane * jnp.where(shifted, -bound, 0.0)
    k_bias = shift_lane

    h = _rmsnorm(x, g_mix, "rmsnorm_mix")

    tm = _tile(s, 1024)

    sb_scale = -1.0 / np.sqrt(SB_HEAD_DIM).astype(np.float32)
    colscale = jnp.concatenate([jnp.full((SB_W,), sb_scale, F32), jnp.ones((2 * SB_W,), F32)]).reshape(1, c0)
    tn = _tile(c0, 768)
    plan = _side_cast(w_in_t, (s // tm) * (c0 // tn), row0=c4, nrows=3 * d)
    sb_qkv, *casted = _matmul(
        h, w_t, b_transposed=True, n=c0, tm=tm, tn=tn, cast=[plan] if plan else [],
        extra=[(colscale, (1, tn), lambda i, j: (0, j))],
        outs=[((s, c0), BF16, (tm, tn), lambda i, j: (i, j))],
        epilogue=_ep_colscale, name="proj_sb_qkv")
    w_gate_t = casted[0] if plan else w_in_t[c4:].astype(BF16)
    o_sb = _sb_attention(sb_qkv)

    freq, cmask, smask = _rope_tables()
    tml = _tile(s, 512)
    nlat = c3 + LANES - ROPE_DIM - c0
    full = lambda shape: (shape, lambda i, j: (0, 0))
    rows = lambda w: ((tml, w), lambda i, j: (i, 0))
    plan = _side_cast(w_out, s // tml)
    cq_n, ckv_n, krot, cos_t, sin_t, *casted = _matmul(
        h, w_t, b_transposed=True, b_col0=c0, n=nlat, tm=tml, tn=nlat, cast=[plan] if plan else [],
        extra=[(pos, *rows(1)), (freq, *full((1, LANES))), (cmask, *full((1, LANES))), (smask, *full((1, LANES))),
               (g_cq.reshape(1, Q_LORA), *full((1, Q_LORA))), (g_ckv.reshape(1, KV_LORA), *full((1, KV_LORA)))],
        outs=[((s, Q_LORA), BF16, *rows(Q_LORA)), ((s, KV_LORA), BF16, *rows(KV_LORA)),
              ((s, LANES), F32, *rows(LANES)), ((s, LANES), F32, *rows(LANES)), ((s, LANES), F32, *rows(LANES))],
        epilogue=_ep_latent, name="proj_mla_latent")
    w_out_b = casted[0] if plan else w_out.astype(BF16)

    gq_n, gq_r = g_q_mla[:NOPE_DIM].reshape(1, LANES), _pad_gain(g_q_mla[NOPE_DIM:])
    gk_n, gk_r = g_k_mla[:NOPE_DIM].reshape(1, LANES), _pad_gain(g_k_mla[NOPE_DIM:])
    rows = lambda w: ((tm, w), lambda i, j: (i, 0))
    mla_scale = np.float32(np.log2(np.e) / np.sqrt(QK_DIM))
    tnh = MLA_HEADS_PER_STEP * 2 * LANES
    (q_mla,) = _matmul(
        cq_n, w_qb, tm=tm, tn=MLA_HEADS_PER_STEP * 3 * LANES,
        extra=[(cos_t, *rows(LANES)), (sin_t, *rows(LANES)), (gq_n, *full((1, LANES))), (gq_r, *full((1, LANES))),
               (q_bias, *full((1, LANES)))],
        outs=[((s, MLA_HEADS * 2 * LANES), BF16, (tm, tnh), lambda i, j: (i, j))],
        epilogue=functools.partial(_ep_mla_q, mla_scale), name="mla_q_up")
    k_mla, v_mla = _matmul(
        ckv_n, w_kvb, tm=tm, tn=tnh,
        extra=[(krot, *rows(LANES)), (gk_n, *full((1, LANES))), (gk_r, *full((1, LANES))),
               (k_bias, *full((1, LANES)))],
        outs=[((s, MLA_HEADS * 2 * LANES), BF16, (tm, tnh), lambda i, j: (i, j)),
              ((s, MLA_HEADS * 2 * LANES), BF16, (tm, tnh), lambda i, j: (i, j))],
        epilogue=_ep_mla_kv, name="mla_kv_up")
    o_mla = _mla_attention(shifted.astype(jnp.int32).reshape(1), q_mla, k_mla, v_mla)

    mkv = _memkv(mem, g_mem, w_mem_kv.astype(BF16), g_k_mem)
    n_mem = mem.shape[0]
    tmm = _tile(s, 512)
    branch_w = [w_sb_o, w_mla_o, w_mem_o]
    plans = [_side_cast(w, s // tmm) for w in branch_w]
    plans = plans if all(plans) else []
    o_mem, *casted = _matmul(
        h, w_t, b_transposed=True, b_col0=c3, n=MEM_W, tm=tmm, tn=MEM_W, cast=plans,
        extra=[(mkv, (n_mem, MEM_W), lambda i, j: (0, 0)), (mkv, (n_mem, MEM_W), lambda i, j: (0, 1)),
               (g_q_mem.reshape(1, MEM_HEAD_DIM), (1, MEM_HEAD_DIM), lambda i, j: (0, 0))],
        outs=[((s, MEM_W), BF16, (tmm, MEM_W), lambda i, j: (i, 0))],
        epilogue=_ep_mem_attn, name="mem_attention")
    w_sb_o_b, w_mla_o_b, w_mem_o_b = casted if plans else [w.astype(BF16) for w in branch_w]

    tng = _tile(3 * d, 1024)
    g_col = g_ffn.reshape(d, 1)
    plan = _side_cast(w_ff1, (s // tm) * (3 * d // tng), scale=g_col)
    gates, *casted = _matmul(
        h, w_gate_t, b_transposed=True, tm=tm, tn=tng, cast=[plan] if plan else [],
        outs=[((s, 3 * d), BF16, (tm, tng), lambda i, j: (i, j))],
        epilogue=_ep_sigmoid, name="proj_gates")
    w_ff1_g = casted[0] if plan else (w_ff1 * g_col).astype(BF16)
    merged = _merge(o_sb, o_mla, o_mem, w_sb_o_b, w_mla_o_b, w_mem_o_b, gates)
    tnd = _tile(d, 512)
    x1, x1_b, x1_ss = _matmul(
        merged, w_out_b, tm=tm, tn=tnd, semantics=("parallel", "arbitrary"),
        extra=[(x, (tm, tnd), lambda i, j: (i, j))],
        outs=[((s, d), F32, (tm, tnd), lambda i, j: (i, j)), ((s, d), BF16, (tm, tnd), lambda i, j: (i, j)),
              ((s, LANES), F32, (tm, LANES), lambda i, j: (i, 0))],
        epilogue=_ep_residual_stats, name="out_proj")

    dff = w_ff1.shape[1]
    tnf = _tile(dff, 1024)
    plan = _side_cast(w_ff2, (s // tm) * (dff // tnf))
    u2, *casted = _matmul(
        x1_b, w_ff1_g, tm=tm, tn=tnf, cast=[plan] if plan else [],
        extra=[(x1_ss, (tm, LANES), lambda i, j: (i, 0))],
        outs=[((s, dff), BF16, (tm, tnf), lambda i, j: (i, j))],
        epilogue=functools.partial(_ep_relu2_rms, d), name="ffn_up")
    w_ff2_b = casted[0] if plan else w_ff2.astype(BF16)
    return _matmul_kacc_residual(u2, w_ff2_b, x1, "ffn_down")


def kernel(x, mem, positions, g_mix, g_mem, w_in, g_cq, g_ckv, w_q_b, w_kv_b, g_q_mla, g_k_mla, w_mem_kv, g_q_mem,
           g_k_mem, w_sb_o, w_mla_o, w_mem_o, w_out, g_ffn, w_ff1, w_ff2):
    depth = w_in.shape[0]
    assert x.shape[0] == 1 and mem.shape[0] == 1
    y = x[0]
    for i in range(depth):
        y = _layer(y, mem[0], positions[0], g_mix[i], g_mem[i], w_in[i], g_cq[i], g_ckv[i], w_q_b[i], w_kv_b[i],
                   g_q_mla[i], g_k_mla[i], w_mem_kv[i], g_q_mem[i], g_k_mem[i], w_sb_o[i], w_mla_o[i], w_mem_o[i],
                   w_out[i], g_ffn[i], w_ff1[i], w_ff2[i])
    return y[None]
```

```python
import functools
from typing import Any, NamedTuple

import jax
import jax.numpy as jnp
import numpy as np
from jax import lax
from jax.experimental import pallas as pl
from jax.experimental.pallas import tpu as pltpu

F32 = jnp.float32
BF16 = jnp.bfloat16

CHUNK = 64
SB_HEADS = 12
SB_HEAD_DIM = 128
SB_W = SB_HEADS * SB_HEAD_DIM
MLA_HEADS = 12
Q_LORA = 896
KV_LORA = 512
NOPE_DIM = 128
ROPE_DIM = 64
V_DIM = 128
QK_DIM = NOPE_DIM + ROPE_DIM
MEM_HEADS = 4
MEM_HEAD_DIM = 256
MEM_W = MEM_HEADS * MEM_HEAD_DIM
ROPE_THETA = 10000.0
EPS = 1e-6

LANES = 128
BF16_SUBLANES = 16
F32_SUBLANES = 8
SIDE_CAST_MAX_BYTES = 4 * 1024 * 1024
VMEM_LIMIT_BYTES = 60 * 1024 * 1024
SB_DEAD_LOG = -120.0
ATTN_TILE = 256
SB_HEADS_PER_STEP = 4
MLA_TILE = 512
MLA_SHIFT_BOUND = 50.0
MLA_HEADS_PER_STEP = 4


def _params(*sem):
    return pltpu.CompilerParams(dimension_semantics=sem, vmem_limit_bytes=VMEM_LIMIT_BYTES)


def _tile(n, pref):
    t = int(min(n, pref))
    assert n % t == 0, (n, t)
    return t


def _rmsnorm_kernel(x_ref, g_ref, o_ref):
    x = x_ref[...]
    ms = jnp.mean(x * x, axis=-1, keepdims=True)
    o_ref[...] = (x * lax.rsqrt(ms + EPS) * g_ref[...]).astype(o_ref.dtype)


def _rmsnorm(x, g, name):
    m, d = x.shape
    tm = _tile(m, 256)
    return pl.pallas_call(
        _rmsnorm_kernel,
        grid=(m // tm,),
        in_specs=[pl.BlockSpec((tm, d), lambda i: (i, 0)), pl.BlockSpec((1, d), lambda i: (0, 0))],
        out_specs=pl.BlockSpec((tm, d), lambda i: (i, 0)),
        out_shape=jax.ShapeDtypeStruct((m, d), BF16),
        compiler_params=_params("parallel"),
        name=name,
    )(x, g.reshape(1, d))


def _cast_kernel(w_ref, o_ref):
    o_ref[...] = w_ref[...].astype(o_ref.dtype)


def _cast_bf16(w, name, rows=None):
    cols = w.shape[1]
    rows = w.shape[0] if rows is None else rows
    assert rows % BF16_SUBLANES == 0
    units = rows // BF16_SUBLANES
    tr = BF16_SUBLANES * max(u for u in range(1, 129) if units % u == 0)
    tc = _tile(cols, 1024)
    spec = pl.BlockSpec((tr, tc), lambda i, j: (i, j))
    return pl.pallas_call(
        _cast_kernel,
        grid=(rows // tr, cols // tc),
        in_specs=[spec],
        out_specs=spec,
        out_shape=jax.ShapeDtypeStruct((rows, cols), BF16),
        compiler_params=_params("parallel", "parallel"),
        name=name,
    )(w)


class _SideCast(NamedTuple):
    w: jax.Array
    row0: int
    nrows: int
    slab: int
    scale: Any = None


def _side_cast(w, steps, row0=0, nrows=None, scale=None):
    nrows = w.shape[0] - row0 if nrows is None else nrows
    if row0 % F32_SUBLANES:
        return None
    for slab in range(BF16_SUBLANES, nrows + 1, BF16_SUBLANES):
        if nrows % slab == 0 and nrows // slab <= steps:
            fits = slab * w.shape[1] * 4 <= SIDE_CAST_MAX_BYTES
            return _SideCast(w, row0, nrows, slab, scale) if fits else None
    return None


def _mm_kernel(epilogue, n_extra, cast_has_scale, b_transposed, a_ref, b_ref, *refs):
    n_cast = len(cast_has_scale)
    n_scale = sum(cast_has_scale)
    n_in = n_extra + n_cast + n_scale
    extra, cast_in, scales = refs[:n_extra], refs[n_extra:n_extra + n_cast], list(refs[n_extra + n_cast:n_in])
    outs, cast_out = refs[n_in:len(refs) - n_cast], refs[len(refs) - n_cast:]
    if b_transposed:
        acc = lax.dot_general(a_ref[...], b_ref[...], (((1,), (1,)), ((), ())), preferred_element_type=F32)
    else:
        acc = jnp.dot(a_ref[...], b_ref[...], preferred_element_type=F32)
    epilogue(acc, extra, outs)
    for src_ref, dst_ref, has_scale in zip(cast_in, cast_out, cast_has_scale):
        w = src_ref[...]
        if has_scale:
            w = w * scales.pop(0)[...]
        dst_ref[...] = w.astype(dst_ref.dtype)


def _matmul(a, b, *, tm, tn, extra=(), outs, epilogue, name, b_col0=0, n=None, cast=(), b_transposed=False,
            semantics=("parallel", "parallel")):
    m, k = a.shape
    k2, n_total = (b.shape[1], b.shape[0]) if b_transposed else b.shape
    n = n_total if n is None else n
    assert k == k2 and m % tm == 0 and n % tn == 0 and b_col0 + n <= n_total
    gm, gn = m // tm, n // tn
    if b_transposed:
        assert b_col0 % BF16_SUBLANES == 0 and tn % BF16_SUBLANES == 0
        b_spec = pl.BlockSpec((pl.Element(tn), pl.Element(k)),
                              lambda i, j: (pl.multiple_of(b_col0 + j * tn, BF16_SUBLANES), 0))
    else:
        assert b_col0 % tn == 0
        jb = b_col0 // tn
        b_spec = pl.BlockSpec((k, tn), lambda i, j: (0, jb + j))
    in_specs = [pl.BlockSpec((tm, k), lambda i, j: (i, 0)), b_spec]
    in_specs += [pl.BlockSpec(bs, im) for _, bs, im in extra]
    out_specs = [pl.BlockSpec(bs, im) for _, _, bs, im in outs]
    out_shape = [jax.ShapeDtypeStruct(s, dt) for s, dt, _, _ in outs]
    scale_specs, scale_args = [], []
    if cast:
        semantics = ("arbitrary", "arbitrary")
    for c in cast:
        n_slabs = c.nrows // c.slab
        assert n_slabs * c.slab == c.nrows and n_slabs <= gm * gn, (name, c.nrows, c.slab, gm, gn)
        cols = c.w.shape[1]

        def slab_index(i, j, n_slabs=n_slabs):
            return jnp.minimum(i * gn + j, n_slabs - 1)

        def src_spec(width, c=c, slab_index=slab_index):
            if c.row0 % c.slab == 0:
                first = c.row0 // c.slab
                return pl.BlockSpec((c.slab, width), lambda i, j: (first + slab_index(i, j), 0))
            return pl.BlockSpec(
                (pl.Element(c.slab), pl.Element(width)),
                lambda i, j: (pl.multiple_of(c.row0 + slab_index(i, j) * c.slab, F32_SUBLANES), 0))

        in_specs.append(src_spec(cols))
        if c.scale is not None:
            scale_specs.append(src_spec(1))
            scale_args.append(c.scale)
        out_specs.append(pl.BlockSpec((c.slab, cols), lambda i, j, slab_index=slab_index: (slab_index(i, j), 0)))
        out_shape.append(jax.ShapeDtypeStruct((c.nrows, cols), BF16))
    res = pl.pallas_call(
        functools.partial(_mm_kernel, epilogue, len(extra), tuple(c.scale is not None for c in cast), b_transposed),
        grid=(gm, gn),
        in_specs=in_specs + scale_specs,
        out_specs=out_specs,
        out_shape=out_shape,
        compiler_params=_params(*semantics),
        name=name,
    )(a, b, *[x for x, _, _ in extra], *[c.w for c in cast], *scale_args)
    return res


def _sumsq(x):
    return jnp.sum(x * x, axis=-1, keepdims=True)


def _sumsq2(x, y):
    return jnp.sum(x * x + y * y, axis=-1, keepdims=True)


def _swap_rope_halves(x):
    half = ROPE_DIM // 2
    lane = lax.broadcasted_iota(jnp.int32, x.shape, 1)
    return jnp.where(lane < half, pltpu.roll(x, LANES - half, 1), pltpu.roll(x, half, 1))


def _ep_colscale(acc, extra, outs):
    (scale_ref,) = extra
    (o_ref,) = outs
    o_ref[...] = (acc * scale_ref[...]).astype(o_ref.dtype)


def _ep_sigmoid(acc, extra, outs):
    (o_ref,) = outs
    o_ref[...] = (0.5 * jnp.tanh(0.5 * acc) + 0.5).astype(o_ref.dtype)


def _ep_latent(acc, extra, outs):
    pos_ref, freq_ref, cmask_ref, smask_ref, gq_ref, gkv_ref = extra
    cq_ref, ckv_ref, krot_ref, cos_ref, sin_ref = outs
    cq = acc[:, :Q_LORA]
    inv = lax.rsqrt(_sumsq(cq) * (1.0 / Q_LORA) + EPS)
    cq_ref[...] = (cq * inv * gq_ref[...]).astype(cq_ref.dtype)
    ckv = acc[:, Q_LORA:Q_LORA + KV_LORA]
    inv = lax.rsqrt(_sumsq(ckv) * (1.0 / KV_LORA) + EPS)
    ckv_ref[...] = (ckv * inv * gkv_ref[...]).astype(ckv_ref.dtype)
    ang = pos_ref[...].astype(F32) * freq_ref[...]
    cos_t = jnp.cos(ang) * cmask_ref[...]
    sin_t = jnp.sin(ang) * smask_ref[...]
    base = Q_LORA + KV_LORA
    ka = acc[:, base:base + LANES]
    kb = _swap_rope_halves(ka)
    krot_ref[...] = ka * cos_t + kb * sin_t
    cos_ref[...] = cos_t
    sin_ref[...] = sin_t


def _ep_mla_q(scale, acc, extra, outs):
    cos_ref, sin_ref, gn_ref, gr_ref, bias_ref = extra
    (o_ref,) = outs
    for hh in range(acc.shape[1] // (3 * LANES)):
        src, dst = hh * 3 * LANES, hh * 2 * LANES
        nope = acc[:, src:src + LANES]
        rot = acc[:, src + LANES:src + 2 * LANES] * cos_ref[...] + acc[:, src + 2 * LANES:src + 3 * LANES] * sin_ref[...]
        inv = lax.rsqrt(_sumsq2(nope, rot) * (1.0 / QK_DIM) + EPS) * scale
        o_ref[:, dst:dst + LANES] = (nope * inv * gn_ref[...]).astype(o_ref.dtype)
        o_ref[:, dst + LANES:dst + 2 * LANES] = (rot * inv * gr_ref[...] + bias_ref[...]).astype(o_ref.dtype)


def _ep_mla_kv(acc, extra, outs):
    krot_ref, gn_ref, gr_ref, bias_ref = extra
    k_ref, v_ref = outs
    rot = krot_ref[...]
    for base in range(0, acc.shape[1], 2 * LANES):
        nope = acc[:, base:base + LANES]
        inv = lax.rsqrt(_sumsq2(nope, rot) * (1.0 / QK_DIM) + EPS)
        k_ref[:, base:base + LANES] = (nope * inv * gn_ref[...]).astype(k_ref.dtype)
        k_ref[:, base + LANES:base + 2 * LANES] = (rot * inv * gr_ref[...] + bias_ref[...]).astype(k_ref.dtype)
        v_ref[:, base:base + LANES] = acc[:, base + LANES:base + 2 * LANES].astype(v_ref.dtype)
        v_ref[:, base + LANES:base + 2 * LANES] = jnp.ones((acc.shape[0], LANES), v_ref.dtype)


def _ep_mem_attn(acc, extra, outs):
    mk_ref, mv_ref, gq_ref = extra
    (o_ref,) = outs
    scale = 1.0 / np.sqrt(MEM_HEAD_DIM)
    for h in range(MEM_HEADS):
        sl = slice(h * MEM_HEAD_DIM, (h + 1) * MEM_HEAD_DIM)
        qh = acc[:, sl]
        inv = lax.rsqrt(_sumsq(qh) * (1.0 / MEM_HEAD_DIM) + EPS) * scale
        qn = (qh * inv * gq_ref[...]).astype(BF16)
        s = lax.dot_general(qn, mk_ref[:, sl], (((1,), (1,)), ((), ())), preferred_element_type=F32)
        p = jnp.exp(s - jnp.max(s, axis=-1, keepdims=True))
        den = jnp.sum(p, axis=-1, keepdims=True)
        o = jnp.dot(p.astype(BF16), mv_ref[:, sl], preferred_element_type=F32)
        o_ref[:, sl] = (o / den).astype(o_ref.dtype)


def _ep_relu2_rms(d_model, acc, extra, outs):
    (ss_ref,) = extra
    (o_ref,) = outs
    inv = lax.rsqrt(ss_ref[...] * (1.0 / d_model) + EPS)
    u = jnp.maximum(acc * jnp.concatenate([inv] * (acc.shape[1] // LANES), axis=1), 0.0)
    o_ref[...] = (u * u).astype(o_ref.dtype)


def _ep_residual_stats(acc, extra, outs):
    (r_ref,) = extra
    x1_ref, xb_ref, ss_ref = outs
    x1 = r_ref[...] + acc
    x1_ref[...] = x1
    xb_ref[...] = x1.astype(xb_ref.dtype)
    part = jnp.broadcast_to(_sumsq(x1), ss_ref.shape)

    @pl.when(pl.program_id(1) == 0)
    def _():
        ss_ref[...] = part

    @pl.when(pl.program_id(1) != 0)
    def _():
        ss_ref[...] += part


def _memkv_kernel(mem_ref, g_ref, w_ref, gk_ref, o_ref):
    x = mem_ref[...]
    ms = jnp.mean(x * x, axis=-1, keepdims=True)
    hn = (x * lax.rsqrt(ms + EPS) * g_ref[...]).astype(BF16)
    acc = jnp.dot(hn, w_ref[...], preferred_element_type=F32)
    j = pl.program_id(0)
    inv = lax.rsqrt(_sumsq(acc) * (1.0 / MEM_HEAD_DIM) + EPS)
    normed = acc * inv * gk_ref[...]
    o_ref[...] = jnp.where(j < MEM_HEADS, normed, acc).astype(o_ref.dtype)


def _memkv(mem, g_mem, w_mem_kv_bf16, g_k_mem):
    n_mem, d = mem.shape
    n = w_mem_kv_bf16.shape[1]
    return pl.pallas_call(
        _memkv_kernel,
        grid=(n // MEM_HEAD_DIM,),
        in_specs=[
            pl.BlockSpec((n_mem, d), lambda j: (0, 0)),
            pl.BlockSpec((1, d), lambda j: (0, 0)),
            pl.BlockSpec((d, MEM_HEAD_DIM), lambda j: (0, j)),
            pl.BlockSpec((1, MEM_HEAD_DIM), lambda j: (0, 0)),
        ],
        out_specs=pl.BlockSpec((n_mem, MEM_HEAD_DIM), lambda j: (0, j)),
        out_shape=jax.ShapeDtypeStruct((n_mem, n), BF16),
        compiler_params=_params("parallel"),
        name="mem_kv",
    )(mem, g_mem.reshape(1, d), w_mem_kv_bf16, g_k_mem.reshape(1, MEM_HEAD_DIM))


def _sb_attn_kernel(q_ref, k_ref, v_ref, tri_ref, o_ref, acc_ref, carry_ref):
    t = q_ref.shape[0]
    hd = SB_HEAD_DIM
    i = pl.program_id(1)

    def suffix_sums(l):
        pieces, later = [], None
        for p in reversed(range(l.shape[1] // t)):
            lp = l[:, p * t:(p + 1) * t]
            hi = lp.astype(BF16)
            lo = (lp - hi.astype(F32)).astype(BF16)
            suf = jnp.dot(jnp.concatenate([hi, lo], axis=1), tri_ref[...], preferred_element_type=F32)
            total = suf[:, 0:1] + lp[:, 0:1]
            if later is not None:
                suf = suf + later
                total = total + later
            pieces.insert(0, suf)
            later = total
        return (pieces[0] if len(pieces) == 1 else jnp.concatenate(pieces, axis=1)), later

    def block(head, start, width, mask_offset):
        cols = slice(head * hd, (head + 1) * hd)
        k = k_ref[pl.ds(start, width), cols]
        v = v_ref[pl.ds(start, width), cols]
        nz = lax.dot_general(q_ref[:, cols], k, (((1,), (1,)), ((), ())), preferred_element_type=F32)
        l = jnp.minimum(nz, 0.0) - jnp.log(1.0 + jnp.exp(-jnp.abs(nz)))
        if mask_offset is not None:
            row = lax.broadcasted_iota(jnp.int32, (t, width), 0)
            col = lax.broadcasted_iota(jnp.int32, (t, width), 1)
            causal = col < row + mask_offset
            l = jnp.where(causal, l, 0.0)
        suffix, total = suffix_sums(l)
        carry = carry_ref[:, cols]
        a = jnp.exp((l - nz) + suffix + jnp.concatenate([carry] * (width // LANES), axis=1))
        if mask_offset is not None:
            a = jnp.where(causal, a, 0.0)
        acc_ref[:, cols] += jnp.dot(a.astype(BF16), v, preferred_element_type=F32)
        new_carry = carry + jnp.broadcast_to(total, carry.shape)
        carry_ref[:, cols] = new_carry
        return jnp.max(new_carry)

    def blocks(start, width, mask_offset):
        tops = [block(head, start, width, mask_offset) for head in range(SB_HEADS_PER_STEP)]
        return functools.reduce(jnp.maximum, tops)

    acc_ref[...] = jnp.zeros_like(acc_ref)
    carry_ref[...] = jnp.zeros_like(carry_ref)
    first = jnp.maximum(i - 1, 0)
    top = blocks(pl.multiple_of(first * t, t), 2 * t, (i - first) * t)

    def cond(state):
        j, top = state
        return jnp.logical_and(j >= 0, top > SB_DEAD_LOG)

    def body(state):
        j, _ = state
        return j - 1, blocks(pl.multiple_of(j * t, t), t, None)

    lax.while_loop(cond, body, (i - 2, top))
    o_ref[...] = acc_ref[...].astype(o_ref.dtype)


def _suffix_ones(width):
    tri = (np.arange(width)[:, None] > np.arange(width)[None, :]).astype(np.float32)
    return jnp.asarray(np.concatenate([tri, tri], axis=0), dtype=BF16)


def _sb_attention(qkv):
    s = qkv.shape[0]
    t = _tile(s, ATTN_TILE)
    assert s >= 2 * t
    w = SB_HEADS_PER_STEP * SB_HEAD_DIM
    groups = SB_HEADS // SB_HEADS_PER_STEP
    return pl.pallas_call(
        _sb_attn_kernel,
        grid=(groups, s // t),
        in_specs=[
            pl.BlockSpec((t, w), lambda g, i: (i, g)),
            pl.BlockSpec((s, w), lambda g, i: (0, groups + g)),
            pl.BlockSpec((s, w), lambda g, i: (0, 2 * groups + g)),
            pl.BlockSpec((2 * t, t), lambda g, i: (0, 0)),
        ],
        out_specs=pl.BlockSpec((t, w), lambda g, i: (i, g)),
        out_shape=jax.ShapeDtypeStruct((s, SB_W), BF16),
        scratch_shapes=[pltpu.VMEM((t, w), F32), pltpu.VMEM((t, w), F32)],
        compiler_params=_params("parallel", "parallel"),
        name="sb_attention",
    )(qkv, qkv, qkv, _suffix_ones(t))


def _mla_attn_kernel(fast_ref, q_ref, k_ref, v_ref, o_ref, acc_ref, m_ref):
    tq = q_ref.shape[0]
    t = tq // 2
    i = pl.program_id(1)
    q = q_ref[...]

    def scores(j, chunk_offset):
        start = pl.multiple_of(j * t, t)
        s = lax.dot_general(q, k_ref[pl.ds(start, t), :], (((1,), (1,)), ((), ())), preferred_element_type=F32)
        if chunk_offset is not None:
            row = lax.broadcasted_iota(jnp.int32, (tq, t), 0)
            col = lax.broadcasted_iota(jnp.int32, (tq, t), 1)
            s = jnp.where((col // CHUNK) + chunk_offset <= (row // CHUNK), s, -jnp.inf)
        return s, v_ref[pl.ds(start, t), :]

    def shifted_tile(j, chunk_offset):
        s, v = scores(j, chunk_offset)
        acc_ref[...] += jnp.dot(jnp.exp2(s).astype(BF16), v, preferred_element_type=F32)

    def online_tile(j, chunk_offset):
        s, v = scores(j, chunk_offset)
        m_prev = m_ref[...]
        m_new = jnp.maximum(m_prev, jnp.max(s, axis=-1, keepdims=True))
        alpha = jnp.exp2(m_prev - m_new)
        p = jnp.exp2(s - jnp.concatenate([m_new] * (t // LANES), axis=1))
        acc_ref[...] = jnp.concatenate([alpha, alpha], axis=1) * acc_ref[...] + jnp.dot(
            p.astype(BF16), v, preferred_element_type=F32)
        m_ref[...] = m_new

    def sweep(tile):
        tile(2 * i, 0)
        tile(2 * i + 1, t // CHUNK)

        def quad(p, c):
            for u in range(4):
                tile(4 * p + u, None)
            return c

        lax.fori_loop(0, i // 2, quad, 0)

        @pl.when(i % 2 == 1)
        def _():
            tile(2 * i - 2, None)
            tile(2 * i - 1, None)

    acc_ref[...] = jnp.zeros_like(acc_ref)
    shifted = fast_ref[0] == 1

    @pl.when(shifted)
    def _():
        sweep(shifted_tile)

    @pl.when(jnp.logical_not(shifted))
    def _():
        m_ref[...] = jnp.full_like(m_ref, -jnp.inf)
        sweep(online_tile)

    acc = acc_ref[...]
    o_ref[...] = (acc[:, :V_DIM] / acc[:, V_DIM:]).astype(o_ref.dtype)


def _mla_attention(shifted, q, k, v_ones):
    s = q.shape[0]
    t = _tile(s, 2 * MLA_TILE)
    dk = 2 * LANES
    return pl.pallas_call(
        _mla_attn_kernel,
        grid_spec=pltpu.PrefetchScalarGridSpec(
            num_scalar_prefetch=1,
            grid=(MLA_HEADS, s // t),
            in_specs=[
                pl.BlockSpec((t, dk), lambda h, i, f: (i, h)),
                pl.BlockSpec((s, dk), lambda h, i, f: (0, h)),
                pl.BlockSpec((s, dk), lambda h, i, f: (0, h)),
            ],
            out_specs=pl.BlockSpec((t, V_DIM), lambda h, i, f: (i, h)),
            scratch_shapes=[pltpu.VMEM((t, 2 * V_DIM), F32), pltpu.VMEM((t, LANES), F32)],
        ),
        out_shape=jax.ShapeDtypeStruct((s, MLA_HEADS * V_DIM), BF16),
        compiler_params=_params("parallel", "parallel"),
        name="mla_attention",
    )(shifted, q, k, v_ones)


def _merge_kernel(osb_ref, omla_ref, omem_ref, wsb_ref, wmla_ref, wmem_ref, g0_ref, g1_ref, g2_ref, o_ref):
    acc = g0_ref[...].astype(F32) * jnp.dot(osb_ref[...], wsb_ref[...], preferred_element_type=F32)
    acc += g1_ref[...].astype(F32) * jnp.dot(omla_ref[...], wmla_ref[...], preferred_element_type=F32)
    acc += g2_ref[...].astype(F32) * jnp.dot(omem_ref[...], wmem_ref[...], preferred_element_type=F32)
    o_ref[...] = acc.astype(o_ref.dtype)


def _merge(o_sb, o_mla, o_mem, w_sb_o, w_mla_o, w_mem_o, gates):
    m = o_sb.shape[0]
    d = w_sb_o.shape[1]
    tm = _tile(m, 1024)
    tn = _tile(d, 1024)
    nj = d // tn
    row = lambda i, j: (i, 0)
    colw = lambda i, j: (0, j)
    return pl.pallas_call(
        _merge_kernel,
        grid=(m // tm, nj),
        in_specs=[
            pl.BlockSpec((tm, o_sb.shape[1]), row),
            pl.BlockSpec((tm, o_mla.shape[1]), row),
            pl.BlockSpec((tm, o_mem.shape[1]), row),
            pl.BlockSpec((w_sb_o.shape[0], tn), colw),
            pl.BlockSpec((w_mla_o.shape[0], tn), colw),
            pl.BlockSpec((w_mem_o.shape[0], tn), colw),
            pl.BlockSpec((tm, tn), lambda i, j: (i, j)),
            pl.BlockSpec((tm, tn), lambda i, j: (i, nj + j)),
            pl.BlockSpec((tm, tn), lambda i, j: (i, 2 * nj + j)),
        ],
        out_specs=pl.BlockSpec((tm, tn), lambda i, j: (i, j)),
        out_shape=jax.ShapeDtypeStruct((m, d), BF16),
        compiler_params=_params("parallel", "parallel"),
        name="gated_merge",
    )(o_sb, o_mla, o_mem, w_sb_o, w_mla_o, w_mem_o, gates, gates, gates)


def _mm_kacc_kernel(a_ref, b_ref, r_ref, o_ref):
    @pl.when(pl.program_id(2) == 0)
    def _():
        o_ref[...] = r_ref[...]

    o_ref[...] += jnp.dot(a_ref[...], b_ref[...], preferred_element_type=F32)


def _matmul_kacc_residual(a, b, r, name):
    m, k = a.shape
    n = b.shape[1]
    tm, tn, tk = _tile(m, 1024), _tile(n, 1024), _tile(k, 4096)
    return pl.pallas_call(
        _mm_kacc_kernel,
        grid=(m // tm, n // tn, k // tk),
        in_specs=[
            pl.BlockSpec((tm, tk), lambda i, j, kk: (i, kk)),
            pl.BlockSpec((tk, tn), lambda i, j, kk: (kk, j)),
            pl.BlockSpec((tm, tn), lambda i, j, kk: (i, j)),
        ],
        out_specs=pl.BlockSpec((tm, tn), lambda i, j, kk: (i, j)),
        out_shape=jax.ShapeDtypeStruct((m, n), F32),
        compiler_params=_params("parallel", "parallel", "arbitrary"),
        name=name,
    )(a, b, r)


def _rope_tables():
    half = ROPE_DIM // 2
    freqs = 1.0 / (ROPE_THETA ** (jnp.arange(half, dtype=F32) / half))
    zeros = jnp.zeros((LANES - ROPE_DIM,), F32)
    freq = jnp.concatenate([freqs, freqs, zeros]).reshape(1, LANES)
    cmask = jnp.concatenate([jnp.ones((ROPE_DIM,), F32), zeros]).reshape(1, LANES)
    smask = jnp.concatenate([-jnp.ones((half,), F32), jnp.ones((half,), F32), zeros]).reshape(1, LANES)
    return freq, cmask, smask


def _pad_gain(g):
    return jnp.concatenate([g, jnp.zeros((LANES - g.shape[0],), g.dtype)]).reshape(1, LANES)


def _layer(x, mem, positions, g_mix, g_mem, w_in, g_cq, g_ckv, w_q_b, w_kv_b, g_q_mla, g_k_mla, w_mem_kv,
           g_q_mem, g_k_mem, w_sb_o, w_mla_o, w_mem_o, w_out, g_ffn, w_ff1, w_ff2):
    s, d = x.shape
    pos = positions.reshape(s, 1)

    c0 = 3 * SB_W
    c1 = c0 + Q_LORA
    c2 = c1 + KV_LORA
    c3 = c2 + ROPE_DIM
    c4 = c3 + MEM_W
    w_in_t = jnp.transpose(w_in)
    w_t = _cast_bf16(w_in_t, "cast_w_in", rows=c4)
    half = ROPE_DIM // 2
    wq = w_q_b.reshape(Q_LORA, MLA_HEADS, QK_DIM)
    wq_pad = jnp.zeros((Q_LORA, MLA_HEADS, LANES - ROPE_DIM), wq.dtype)
    w_qb = jnp.concatenate([wq, wq_pad, wq[..., NOPE_DIM + half:], wq[..., NOPE_DIM:NOPE_DIM + half], wq_pad], axis=-1)
    w_qb = w_qb.reshape(Q_LORA, MLA_HEADS * 3 * LANES).astype(BF16)
    w_kvb = w_kv_b.astype(BF16)

    bound = np.float32(np.sqrt(QK_DIM) * np.log2(np.e)) * jnp.max(jnp.abs(g_q_mla)) * jnp.max(jnp.abs(g_k_mla))
    shifted = bound <= MLA_SHIFT_BOUND
    shift_lane = (jnp.arange(LANES) == ROPE_DIM).astype(F32).reshape(1, LANES)
    q_bias = shift_lane * jnp.where(shifted, -bound, 0.0)
    k_bias = shift_lane

    h = _rmsnorm(x, g_mix, "rmsnorm_mix")

    tm = _tile(s, 1024)

    sb_scale = -1.0 / np.sqrt(SB_HEAD_DIM).astype(np.float32)
    colscale = jnp.concatenate([jnp.full((SB_W,), sb_scale, F32), jnp.ones((2 * SB_W,), F32)]).reshape(1, c0)
    tn = _tile(c0, 768)
    plan = _side_cast(w_in_t, (s // tm) * (c0 // tn), row0=c4, nrows=3 * d)
    sb_qkv, *casted = _matmul(
        h, w_t, b_transposed=True, n=c0, tm=tm, tn=tn, cast=[plan] if plan else [],
        extra=[(colscale, (1, tn), lambda i, j: (0, j))],
        outs=[((s, c0), BF16, (tm, tn), lambda i, j: (i, j))],
        epilogue=_ep_colscale, name="proj_sb_qkv")
    w_gate_t = casted[0] if plan else w_in_t[c4:].astype(BF16)
    o_sb = _sb_attention(sb_qkv)

    freq, cmask, smask = _rope_tables()
    tml = _tile(s, 512)
    nlat = c3 + LANES - ROPE_DIM - c0
    full = lambda shape: (shape, lambda i, j: (0, 0))
    rows = lambda w: ((tml, w), lambda i, j: (i, 0))
    plan = _side_cast(w_out, s // tml)
    cq_n, ckv_n, krot, cos_t, sin_t, *casted = _matmul(
        h, w_t, b_transposed=True, b_col0=c0, n=nlat, tm=tml, tn=nlat, cast=[plan] if plan else [],
        extra=[(pos, *rows(1)), (freq, *full((1, LANES))), (cmask, *full((1, LANES))), (smask, *full((1, LANES))),
               (g_cq.reshape(1, Q_LORA), *full((1, Q_LORA))), (g_ckv.reshape(1, KV_LORA), *full((1, KV_LORA)))],
        outs=[((s, Q_LORA), BF16, *rows(Q_LORA)), ((s, KV_LORA), BF16, *rows(KV_LORA)),
              ((s, LANES), F32, *rows(LANES)), ((s, LANES), F32, *rows(LANES)), ((s, LANES), F32, *rows(LANES))],
        epilogue=_ep_latent, name="proj_mla_latent")
    w_out_b = casted[0] if plan else w_out.astype(BF16)

    gq_n, gq_r = g_q_mla[:NOPE_DIM].reshape(1, LANES), _pad_gain(g_q_mla[NOPE_DIM:])
    gk_n, gk_r = g_k_mla[:NOPE_DIM].reshape(1, LANES), _pad_gain(g_k_mla[NOPE_DIM:])
    rows = lambda w: ((tm, w), lambda i, j: (i, 0))
    mla_scale = np.float32(np.log2(np.e) / np.sqrt(QK_DIM))
    tnh = MLA_HEADS_PER_STEP * 2 * LANES
    (q_mla,) = _matmul(
        cq_n, w_qb, tm=tm, tn=MLA_HEADS_PER_STEP * 3 * LANES,
        extra=[(cos_t, *rows(LANES)), (sin_t, *rows(LANES)), (gq_n, *full((1, LANES))), (gq_r, *full((1, LANES))),
               (q_bias, *full((1, LANES)))],
        outs=[((s, MLA_HEADS * 2 * LANES), BF16, (tm, tnh), lambda i, j: (i, j))],
        epilogue=functools.partial(_ep_mla_q, mla_scale), name="mla_q_up")
    k_mla, v_mla = _matmul(
        ckv_n, w_kvb, tm=tm, tn=tnh,
        extra=[(krot, *rows(LANES)), (gk_n, *full((1, LANES))), (gk_r, *full((1, LANES))),
               (k_bias, *full((1, LANES)))],
        outs=[((s, MLA_HEADS * 2 * LANES), BF16, (tm, tnh), lambda i, j: (i, j)),
              ((s, MLA_HEADS * 2 * LANES), BF16, (tm, tnh), lambda i, j: (i, j))],
        epilogue=_ep_mla_kv, name="mla_kv_up")
    o_mla = _mla_attention(shifted.astype(jnp.int32).reshape(1), q_mla, k_mla, v_mla)

    mkv = _memkv(mem, g_mem, w_mem_kv.astype(BF16), g_k_mem)
    n_mem = mem.shape[0]
    tmm = _tile(s, 512)
    branch_w = [w_sb_o, w_mla_o, w_mem_o]
    plans = [_side_cast(w, s // tmm) for w in branch_w]
    plans = plans if all(plans) else []
    o_mem, *casted = _matmul(
        h, w_t, b_transposed=True, b_col0=c3, n=MEM_W, tm=tmm, tn=MEM_W, cast=plans,
        extra=[(mkv, (n_mem, MEM_W), lambda i, j: (0, 0)), (mkv, (n_mem, MEM_W), lambda i, j: (0, 1)),
               (g_q_mem.reshape(1, MEM_HEAD_DIM), (1, MEM_HEAD_DIM), lambda i, j: (0, 0))],
        outs=[((s, MEM_W), BF16, (tmm, MEM_W), lambda i, j: (i, 0))],
        epilogue=_ep_mem_attn, name="mem_attention")
    w_sb_o_b, w_mla_o_b, w_mem_o_b = casted if plans else [w.astype(BF16) for w in branch_w]

    tng = _tile(3 * d, 1024)
    g_col = g_ffn.reshape(d, 1)
    plan = _side_cast(w_ff1, (s // tm) * (3 * d // tng), scale=g_col)
    gates, *casted = _matmul(
        h, w_gate_t, b_transposed=True, tm=tm, tn=tng, cast=[plan] if plan else [],
        outs=[((s, 3 * d), BF16, (tm, tng), lambda i, j: (i, j))],
        epilogue=_ep_sigmoid, name="proj_gates")
    w_ff1_g = casted[0] if plan else (w_ff1 * g_col).astype(BF16)
    merged = _merge(o_sb, o_mla, o_mem, w_sb_o_b, w_mla_o_b, w_mem_o_b, gates)
    tnd = _tile(d, 512)
    x1, x1_b, x1_ss = _matmul(
        merged, w_out_b, tm=tm, tn=tnd, semantics=("parallel", "arbitrary"),
        extra=[(x, (tm, tnd), lambda i, j: (i, j))],
        outs=[((s, d), F32, (tm, tnd), lambda i, j: (i, j)), ((s, d), BF16, (tm, tnd), lambda i, j: (i, j)),
              ((s, LANES), F32, (tm, LANES), lambda i, j: (i, 0))],
        epilogue=_ep_residual_stats, name="out_proj")

    dff = w_ff1.shape[1]
    tnf = _tile(dff, 1024)
    plan = _side_cast(w_ff2, (s // tm) * (dff // tnf))
    u2, *casted = _matmul(
        x1_b, w_ff1_g, tm=tm, tn=tnf, cast=[plan] if plan else [],
        extra=[(x1_ss, (tm, LANES), lambda i, j: (i, 0))],
        outs=[((s, dff), BF16, (tm, tnf), lambda i, j: (i, j))],
        epilogue=functools.partial(_ep_relu2_rms, d), name="ffn_up")
    w_ff2_b = casted[0] if plan else w_ff2.astype(BF16)
    return _matmul_kacc_residual(u2, w_ff2_b, x1, "ffn_down")


def kernel(x, mem, positions, g_mix, g_mem, w_in, g_cq, g_ckv, w_q_b, w_kv_b, g_q_mla, g_k_mla, w_mem_kv, g_q_mem,
           g_k_mem, w_sb_o, w_mla_o, w_mem_o, w_out, g_ffn, w_ff1, w_ff2):
    depth = w_in.shape[0]
    assert x.shape[0] == 1 and mem.shape[0] == 1
    y = x[0]
    for i in range(depth):
        y = _layer(y, mem[0], positions[0], g_mix[i], g_mem[i], w_in[i], g_cq[i], g_ckv[i], w_q_b[i], w_kv_b[i],
                   g_q_mla[i], g_k_mla[i], w_mem_kv[i], g_q_mem[i], g_k_mem[i], w_sb_o[i], w_mla_o[i], w_mem_o[i],
                   w_out[i], g_ffn[i], w_ff1[i], w_ff2[i])
    return y[None]
```

```python
import functools
from typing import Any, NamedTuple

import jax
import jax.numpy as jnp
import numpy as np
from jax import lax
from jax.experimental import pallas as pl
from jax.experimental.pallas import tpu as pltpu

F32 = jnp.float32
BF16 = jnp.bfloat16

CHUNK = 64
SB_HEADS = 12
SB_HEAD_DIM = 128
SB_W = SB_HEADS * SB_HEAD_DIM
MLA_HEADS = 12
Q_LORA = 896
KV_LORA = 512
NOPE_DIM = 128
ROPE_DIM = 64
V_DIM = 128
QK_DIM = NOPE_DIM + ROPE_DIM
MEM_HEADS = 4
MEM_HEAD_DIM = 256
MEM_W = MEM_HEADS * MEM_HEAD_DIM
ROPE_THETA = 10000.0
EPS = 1e-6

LANES = 128
BF16_SUBLANES = 16
F32_SUBLANES = 8
SIDE_CAST_MAX_BYTES = 4 * 1024 * 1024
VMEM_LIMIT_BYTES = 60 * 1024 * 1024
SB_DEAD_LOG2 = -176.0
ATTN_TILE = 256
SB_HEADS_PER_STEP = 4
MLA_TILE = 512
MLA_SHIFT_BOUND = 50.0
MLA_HEADS_PER_STEP = 4


def _params(*sem):
    return pltpu.CompilerParams(dimension_semantics=sem, vmem_limit_bytes=VMEM_LIMIT_BYTES)


def _tile(n, pref):
    t = int(min(n, pref))
    assert n % t == 0, (n, t)
    return t


def _rmsnorm_kernel(x_ref, g_ref, o_ref):
    x = x_ref[...]
    ms = jnp.mean(x * x, axis=-1, keepdims=True)
    o_ref[...] = (x * lax.rsqrt(ms + EPS) * g_ref[...]).astype(o_ref.dtype)


def _rmsnorm(x, g, name):
    m, d = x.shape
    tm = _tile(m, 256)
    return pl.pallas_call(
        _rmsnorm_kernel,
        grid=(m // tm,),
        in_specs=[pl.BlockSpec((tm, d), lambda i: (i, 0)), pl.BlockSpec((1, d), lambda i: (0, 0))],
        out_specs=pl.BlockSpec((tm, d), lambda i: (i, 0)),
        out_shape=jax.ShapeDtypeStruct((m, d), BF16),
        compiler_params=_params("parallel"),
        name=name,
    )(x, g.reshape(1, d))


def _cast_kernel(w_ref, o_ref):
    o_ref[...] = w_ref[...].astype(o_ref.dtype)


def _cast_bf16(w, name, rows=None):
    cols = w.shape[1]
    rows = w.shape[0] if rows is None else rows
    assert rows % BF16_SUBLANES == 0
    units = rows // BF16_SUBLANES
    tr = BF16_SUBLANES * max(u for u in range(1, 129) if units % u == 0)
    tc = _tile(cols, 1024)
    spec = pl.BlockSpec((tr, tc), lambda i, j: (i, j))
    return pl.pallas_call(
        _cast_kernel,
        grid=(rows // tr, cols // tc),
        in_specs=[spec],
        out_specs=spec,
        out_shape=jax.ShapeDtypeStruct((rows, cols), BF16),
        compiler_params=_params("parallel", "parallel"),
        name=name,
    )(w)


class _SideCast(NamedTuple):
    w: jax.Array
    row0: int
    nrows: int
    slab: int
    scale: Any = None


def _side_cast(w, steps, row0=0, nrows=None, scale=None):
    nrows = w.shape[0] - row0 if nrows is None else nrows
    if row0 % F32_SUBLANES:
        return None
    for slab in range(BF16_SUBLANES, nrows + 1, BF16_SUBLANES):
        if nrows % slab == 0 and nrows // slab <= steps:
            fits = slab * w.shape[1] * 4 <= SIDE_CAST_MAX_BYTES
            return _SideCast(w, row0, nrows, slab, scale) if fits else None
    return None


def _mm_kernel(epilogue, n_extra, cast_has_scale, b_transposed, a_ref, b_ref, *refs):
    n_cast = len(cast_has_scale)
    n_scale = sum(cast_has_scale)
    n_in = n_extra + n_cast + n_scale
    extra, cast_in, scales = refs[:n_extra], refs[n_extra:n_extra + n_cast], list(refs[n_extra + n_cast:n_in])
    outs, cast_out = refs[n_in:len(refs) - n_cast], refs[len(refs) - n_cast:]
    if b_transposed:
        acc = lax.dot_general(a_ref[...], b_ref[...], (((1,), (1,)), ((), ())), preferred_element_type=F32)
    else:
        acc = jnp.dot(a_ref[...], b_ref[...], preferred_element_type=F32)
    epilogue(acc, extra, outs)
    for src_ref, dst_ref, has_scale in zip(cast_in, cast_out, cast_has_scale):
        w = src_ref[...]
        if has_scale:
            w = w * scales.pop(0)[...]
        dst_ref[...] = w.astype(dst_ref.dtype)


def _matmul(a, b, *, tm, tn, extra=(), outs, epilogue, name, b_col0=0, n=None, cast=(), b_transposed=False,
            semantics=("parallel", "parallel")):
    m, k = a.shape
    k2, n_total = (b.shape[1], b.shape[0]) if b_transposed else b.shape
    n = n_total if n is None else n
    assert k == k2 and m % tm == 0 and n % tn == 0 and b_col0 + n <= n_total
    gm, gn = m // tm, n // tn
    if b_transposed:
        assert b_col0 % BF16_SUBLANES == 0 and tn % BF16_SUBLANES == 0
        b_spec = pl.BlockSpec((pl.Element(tn), pl.Element(k)),
                              lambda i, j: (pl.multiple_of(b_col0 + j * tn, BF16_SUBLANES), 0))
    else:
        assert b_col0 % tn == 0
        jb = b_col0 // tn
        b_spec = pl.BlockSpec((k, tn), lambda i, j: (0, jb + j))
    in_specs = [pl.BlockSpec((tm, k), lambda i, j: (i, 0)), b_spec]
    in_specs += [pl.BlockSpec(bs, im) for _, bs, im in extra]
    out_specs = [pl.BlockSpec(bs, im) for _, _, bs, im in outs]
    out_shape = [jax.ShapeDtypeStruct(s, dt) for s, dt, _, _ in outs]
    scale_specs, scale_args = [], []
    if cast:
        semantics = ("arbitrary", "arbitrary")
    for c in cast:
        n_slabs = c.nrows // c.slab
        assert n_slabs * c.slab == c.nrows and n_slabs <= gm * gn, (name, c.nrows, c.slab, gm, gn)
        cols = c.w.shape[1]

        def slab_index(i, j, n_slabs=n_slabs):
            return jnp.minimum(i * gn + j, n_slabs - 1)

        def src_spec(width, c=c, slab_index=slab_index):
            if c.row0 % c.slab == 0:
                first = c.row0 // c.slab
                return pl.BlockSpec((c.slab, width), lambda i, j: (first + slab_index(i, j), 0))
            return pl.BlockSpec(
                (pl.Element(c.slab), pl.Element(width)),
                lambda i, j: (pl.multiple_of(c.row0 + slab_index(i, j) * c.slab, F32_SUBLANES), 0))

        in_specs.append(src_spec(cols))
        if c.scale is not None:
            scale_specs.append(src_spec(1))
            scale_args.append(c.scale)
        out_specs.append(pl.BlockSpec((c.slab, cols), lambda i, j, slab_index=slab_index: (slab_index(i, j), 0)))
        out_shape.append(jax.ShapeDtypeStruct((c.nrows, cols), BF16))
    res = pl.pallas_call(
        functools.partial(_mm_kernel, epilogue, len(extra), tuple(c.scale is not None for c in cast), b_transposed),
        grid=(gm, gn),
        in_specs=in_specs + scale_specs,
        out_specs=out_specs,
        out_shape=out_shape,
        compiler_params=_params(*semantics),
        name=name,
    )(a, b, *[x for x, _, _ in extra], *[c.w for c in cast], *scale_args)
    return res


def _sumsq(x):
    return jnp.sum(x * x, axis=-1, keepdims=True)


def _sumsq2(x, y):
    return jnp.sum(x * x + y * y, axis=-1, keepdims=True)


def _swap_rope_halves(x):
    half = ROPE_DIM // 2
    lane = lax.broadcasted_iota(jnp.int32, x.shape, 1)
    return jnp.where(lane < half, pltpu.roll(x, LANES - half, 1), pltpu.roll(x, half, 1))


def _ep_colscale(acc, extra, outs):
    (scale_ref,) = extra
    (o_ref,) = outs
    o_ref[...] = (acc * scale_ref[...]).astype(o_ref.dtype)


def _ep_sigmoid(acc, extra, outs):
    (o_ref,) = outs
    o_ref[...] = (0.5 * jnp.tanh(0.5 * acc) + 0.5).astype(o_ref.dtype)


def _ep_latent(acc, extra, outs):
    pos_ref, freq_ref, cmask_ref, smask_ref, gq_ref, gkv_ref = extra
    cq_ref, ckv_ref, krot_ref, cos_ref, sin_ref = outs
    cq = acc[:, :Q_LORA]
    inv = lax.rsqrt(_sumsq(cq) * (1.0 / Q_LORA) + EPS)
    cq_ref[...] = (cq * inv * gq_ref[...]).astype(cq_ref.dtype)
    ckv = acc[:, Q_LORA:Q_LORA + KV_LORA]
    inv = lax.rsqrt(_sumsq(ckv) * (1.0 / KV_LORA) + EPS)
    ckv_ref[...] = (ckv * inv * gkv_ref[...]).astype(ckv_ref.dtype)
    ang = pos_ref[...].astype(F32) * freq_ref[...]
    cos_t = jnp.cos(ang) * cmask_ref[...]
    sin_t = jnp.sin(ang) * smask_ref[...]
    base = Q_LORA + KV_LORA
    ka = acc[:, base:base + LANES]
    kb = _swap_rope_halves(ka)
    krot_ref[...] = ka * cos_t + kb * sin_t
    cos_ref[...] = cos_t
    sin_ref[...] = sin_t


def _ep_mla_q(scale, acc, extra, outs):
    cos_ref, sin_ref, gn_ref, gr_ref, bias_ref = extra
    (o_ref,) = outs
    for hh in range(acc.shape[1] // (3 * LANES)):
        src, dst = hh * 3 * LANES, hh * 2 * LANES
        nope = acc[:, src:src + LANES]
        rot = acc[:, src + LANES:src + 2 * LANES] * cos_ref[...] + acc[:, src + 2 * LANES:src + 3 * LANES] * sin_ref[...]
        inv = lax.rsqrt(_sumsq2(nope, rot) * (1.0 / QK_DIM) + EPS) * scale
        o_ref[:, dst:dst + LANES] = (nope * inv * gn_ref[...]).astype(o_ref.dtype)
        o_ref[:, dst + LANES:dst + 2 * LANES] = (rot * inv * gr_ref[...] + bias_ref[...]).astype(o_ref.dtype)


def _ep_mla_kv(acc, extra, outs):
    krot_ref, gn_ref, gr_ref, bias_ref = extra
    k_ref, v_ref = outs
    rot = krot_ref[...]
    for base in range(0, acc.shape[1], 2 * LANES):
        nope = acc[:, base:base + LANES]
        inv = lax.rsqrt(_sumsq2(nope, rot) * (1.0 / QK_DIM) + EPS)
        k_ref[:, base:base + LANES] = (nope * inv * gn_ref[...]).astype(k_ref.dtype)
        k_ref[:, base + LANES:base + 2 * LANES] = (rot * inv * gr_ref[...] + bias_ref[...]).astype(k_ref.dtype)
        v_ref[:, base:base + LANES] = acc[:, base + LANES:base + 2 * LANES].astype(v_ref.dtype)
        v_ref[:, base + LANES:base + 2 * LANES] = jnp.ones((acc.shape[0], LANES), v_ref.dtype)


def _ep_mem_attn(acc, extra, outs):
    mk_ref, mv_ref, gq_ref = extra
    (o_ref,) = outs
    scale = 1.0 / np.sqrt(MEM_HEAD_DIM)
    for h in range(MEM_HEADS):
        sl = slice(h * MEM_HEAD_DIM, (h + 1) * MEM_HEAD_DIM)
        qh = acc[:, sl]
        inv = lax.rsqrt(_sumsq(qh) * (1.0 / MEM_HEAD_DIM) + EPS) * scale
        qn = (qh * inv * gq_ref[...]).astype(BF16)
        s = lax.dot_general(qn, mk_ref[:, sl], (((1,), (1,)), ((), ())), preferred_element_type=F32)
        p = jnp.exp(s - jnp.max(s, axis=-1, keepdims=True))
        den = jnp.sum(p, axis=-1, keepdims=True)
        o = jnp.dot(p.astype(BF16), mv_ref[:, sl], preferred_element_type=F32)
        o_ref[:, sl] = (o / den).astype(o_ref.dtype)


def _ep_relu2_rms(d_model, acc, extra, outs):
    (ss_ref,) = extra
    (o_ref,) = outs
    inv = lax.rsqrt(ss_ref[...] * (1.0 / d_model) + EPS)
    u = jnp.maximum(acc * jnp.concatenate([inv] * (acc.shape[1] // LANES), axis=1), 0.0)
    o_ref[...] = (u * u).astype(o_ref.dtype)


def _ep_residual_stats(acc, extra, outs):
    (r_ref,) = extra
    x1_ref, xb_ref, ss_ref = outs
    x1 = r_ref[...] + acc
    x1_ref[...] = x1
    xb_ref[...] = x1.astype(xb_ref.dtype)
    part = jnp.broadcast_to(_sumsq(x1), ss_ref.shape)

    @pl.when(pl.program_id(1) == 0)
    def _():
        ss_ref[...] = part

    @pl.when(pl.program_id(1) != 0)
    def _():
        ss_ref[...] += part


def _memkv_kernel(mem_ref, g_ref, w_ref, gk_ref, o_ref):
    x = mem_ref[...]
    ms = jnp.mean(x * x, axis=-1, keepdims=True)
    hn = (x * lax.rsqrt(ms + EPS) * g_ref[...]).astype(BF16)
    acc = jnp.dot(hn, w_ref[...], preferred_element_type=F32)
    j = pl.program_id(0)
    inv = lax.rsqrt(_sumsq(acc) * (1.0 / MEM_HEAD_DIM) + EPS)
    normed = acc * inv * gk_ref[...]
    o_ref[...] = jnp.where(j < MEM_HEADS, normed, acc).astype(o_ref.dtype)


def _memkv(mem, g_mem, w_mem_kv_bf16, g_k_mem):
    n_mem, d = mem.shape
    n = w_mem_kv_bf16.shape[1]
    return pl.pallas_call(
        _memkv_kernel,
        grid=(n // MEM_HEAD_DIM,),
        in_specs=[
            pl.BlockSpec((n_mem, d), lambda j: (0, 0)),
            pl.BlockSpec((1, d), lambda j: (0, 0)),
            pl.BlockSpec((d, MEM_HEAD_DIM), lambda j: (0, j)),
            pl.BlockSpec((1, MEM_HEAD_DIM), lambda j: (0, 0)),
        ],
        out_specs=pl.BlockSpec((n_mem, MEM_HEAD_DIM), lambda j: (0, j)),
        out_shape=jax.ShapeDtypeStruct((n_mem, n), BF16),
        compiler_params=_params("parallel"),
        name="mem_kv",
    )(mem, g_mem.reshape(1, d), w_mem_kv_bf16, g_k_mem.reshape(1, MEM_HEAD_DIM))


def _sb_attn_kernel(q_ref, k_ref, v_ref, tri_ref, o_ref, acc_ref, carry_ref):
    t = q_ref.shape[0]
    hd = SB_HEAD_DIM
    i = pl.program_id(1)

    def suffix_sums(l):
        pieces, later = [], None
        for p in reversed(range(l.shape[1] // t)):
            lp = l[:, p * t:(p + 1) * t]
            hi = lp.astype(BF16)
            lo = (lp - hi.astype(F32)).astype(BF16)
            suf = jnp.dot(jnp.concatenate([hi, lo], axis=1), tri_ref[...], preferred_element_type=F32)
            total = suf[:, 0:1] + lp[:, 0:1]
            if later is not None:
                suf = suf + later
                total = total + later
            pieces.insert(0, suf)
            later = total
        return (pieces[0] if len(pieces) == 1 else jnp.concatenate(pieces, axis=1)), later

    def block(head, start, width, mask_offset):
        cols = slice(head * hd, (head + 1) * hd)
        k = k_ref[pl.ds(start, width), cols]
        v = v_ref[pl.ds(start, width), cols]
        nz = lax.dot_general(q_ref[:, cols], k, (((1,), (1,)), ((), ())), preferred_element_type=F32)
        l = jnp.minimum(nz, 0.0) - jnp.log2(1.0 + jnp.exp2(-jnp.abs(nz)))
        if mask_offset is not None:
            row = lax.broadcasted_iota(jnp.int32, (t, width), 0)
            col = lax.broadcasted_iota(jnp.int32, (t, width), 1)
            causal = col < row + mask_offset
            l = jnp.where(causal, l, 0.0)
        suffix, total = suffix_sums(l)
        carry = carry_ref[:, cols]
        a = jnp.exp2((l - nz) + suffix + jnp.concatenate([carry] * (width // LANES), axis=1))
        if mask_offset is not None:
            a = jnp.where(causal, a, 0.0)
        acc_ref[:, cols] += jnp.dot(a.astype(BF16), v, preferred_element_type=F32)
        new_carry = carry + jnp.broadcast_to(total, carry.shape)
        carry_ref[:, cols] = new_carry
        return jnp.max(new_carry)

    def blocks(start, width, mask_offset):
        tops = [block(head, start, width, mask_offset) for head in range(SB_HEADS_PER_STEP)]
        return functools.reduce(jnp.maximum, tops)

    acc_ref[...] = jnp.zeros_like(acc_ref)
    carry_ref[...] = jnp.zeros_like(carry_ref)
    first = jnp.maximum(i - 1, 0)
    top = blocks(pl.multiple_of(first * t, t), 2 * t, (i - first) * t)

    def cond(state):
        j, top = state
        return jnp.logical_and(j >= 0, top > SB_DEAD_LOG2)

    def body(state):
        j, _ = state
        return j - 1, blocks(pl.multiple_of(j * t, t), t, None)

    lax.while_loop(cond, body, (i - 2, top))
    o_ref[...] = acc_ref[...].astype(o_ref.dtype)


def _suffix_ones(width):
    tri = (np.arange(width)[:, None] > np.arange(width)[None, :]).astype(np.float32)
    return jnp.asarray(np.concatenate([tri, tri], axis=0), dtype=BF16)


def _sb_attention(qkv):
    s = qkv.shape[0]
    t = _tile(s, ATTN_TILE)
    assert s >= 2 * t
    w = SB_HEADS_PER_STEP * SB_HEAD_DIM
    groups = SB_HEADS // SB_HEADS_PER_STEP
    return pl.pallas_call(
        _sb_attn_kernel,
        grid=(groups, s // t),
        in_specs=[
            pl.BlockSpec((t, w), lambda g, i: (i, g)),
            pl.BlockSpec((s, w), lambda g, i: (0, groups + g)),
            pl.BlockSpec((s, w), lambda g, i: (0, 2 * groups + g)),
            pl.BlockSpec((2 * t, t), lambda g, i: (0, 0)),
        ],
        out_specs=pl.BlockSpec((t, w), lambda g, i: (i, g)),
        out_shape=jax.ShapeDtypeStruct((s, SB_W), BF16),
        scratch_shapes=[pltpu.VMEM((t, w), F32), pltpu.VMEM((t, w), F32)],
        compiler_params=_params("parallel", "parallel"),
        name="sb_attention",
    )(qkv, qkv, qkv, _suffix_ones(t))


def _mla_attn_kernel(fast_ref, q_ref, k_ref, v_ref, o_ref, acc_ref, m_ref):
    tq = q_ref.shape[0]
    t = tq // 2
    i = pl.program_id(1)
    every_row, lower_rows = slice(0, tq), slice(t, tq)

    def scores(j, rows, diagonal):
        start = pl.multiple_of(j * t, t)
        s = lax.dot_general(q_ref[rows, :], k_ref[pl.ds(start, t), :], (((1,), (1,)), ((), ())),
                            preferred_element_type=F32)
        if diagonal:
            row = lax.broadcasted_iota(jnp.int32, s.shape, 0)
            col = lax.broadcasted_iota(jnp.int32, s.shape, 1)
            s = jnp.where((col // CHUNK) <= (row // CHUNK), s, -jnp.inf)
        return s, v_ref[pl.ds(start, t), :]

    def shifted_tile(j, rows=every_row, diagonal=False):
        s, v = scores(j, rows, diagonal)
        acc_ref[rows, :] += jnp.dot(jnp.exp2(s).astype(BF16), v, preferred_element_type=F32)

    def online_tile(j, rows=every_row, diagonal=False):
        s, v = scores(j, rows, diagonal)
        m_prev = m_ref[rows, :]
        m_new = jnp.maximum(m_prev, jnp.max(s, axis=-1, keepdims=True))
        alpha = jnp.exp2(m_prev - m_new)
        p = jnp.exp2(s - jnp.concatenate([m_new] * (t // LANES), axis=1))
        acc_ref[rows, :] = jnp.concatenate([alpha, alpha], axis=1) * acc_ref[rows, :] + jnp.dot(
            p.astype(BF16), v, preferred_element_type=F32)
        m_ref[rows, :] = m_new

    def sweep(tile):
        tile(2 * i, every_row, True)
        tile(2 * i + 1, lower_rows, True)

        def quad(p, c):
            for u in range(4):
                tile(4 * p + u)
            return c

        lax.fori_loop(0, i // 2, quad, 0)

        @pl.when(i % 2 == 1)
        def _():
            tile(2 * i - 2)
            tile(2 * i - 1)

    acc_ref[...] = jnp.zeros_like(acc_ref)
    shifted = fast_ref[0] == 1

    @pl.when(shifted)
    def _():
        sweep(shifted_tile)

    @pl.when(jnp.logical_not(shifted))
    def _():
        m_ref[...] = jnp.full_like(m_ref, -jnp.inf)
        sweep(online_tile)

    acc = acc_ref[...]
    o_ref[...] = (acc[:, :V_DIM] / acc[:, V_DIM:]).astype(o_ref.dtype)


def _mla_attention(shifted, q, k, v_ones):
    s = q.shape[0]
    t = _tile(s, 2 * MLA_TILE)
    dk = 2 * LANES
    return pl.pallas_call(
        _mla_attn_kernel,
        grid_spec=pltpu.PrefetchScalarGridSpec(
            num_scalar_prefetch=1,
            grid=(MLA_HEADS, s // t),
            in_specs=[
                pl.BlockSpec((t, dk), lambda h, i, f: (i, h)),
                pl.BlockSpec((s, dk), lambda h, i, f: (0, h)),
                pl.BlockSpec((s, dk), lambda h, i, f: (0, h)),
            ],
            out_specs=pl.BlockSpec((t, V_DIM), lambda h, i, f: (i, h)),
            scratch_shapes=[pltpu.VMEM((t, 2 * V_DIM), F32), pltpu.VMEM((t, LANES), F32)],
        ),
        out_shape=jax.ShapeDtypeStruct((s, MLA_HEADS * V_DIM), BF16),
        compiler_params=_params("parallel", "parallel"),
        name="mla_attention",
    )(shifted, q, k, v_ones)


def _merge_kernel(osb_ref, omla_ref, omem_ref, wsb_ref, wmla_ref, wmem_ref, g0_ref, g1_ref, g2_ref, o_ref):
    acc = g0_ref[...].astype(F32) * jnp.dot(osb_ref[...], wsb_ref[...], preferred_element_type=F32)
    acc += g1_ref[...].astype(F32) * jnp.dot(omla_ref[...], wmla_ref[...], preferred_element_type=F32)
    acc += g2_ref[...].astype(F32) * jnp.dot(omem_ref[...], wmem_ref[...], preferred_element_type=F32)
    o_ref[...] = acc.astype(o_ref.dtype)


def _merge(o_sb, o_mla, o_mem, w_sb_o, w_mla_o, w_mem_o, gates):
    m = o_sb.shape[0]
    d = w_sb_o.shape[1]
    tm = _tile(m, 1024)
    tn = _tile(d, 1024)
    nj = d // tn
    row = lambda i, j: (i, 0)
    colw = lambda i, j: (0, j)
    return pl.pallas_call(
        _merge_kernel,
        grid=(m // tm, nj),
        in_specs=[
            pl.BlockSpec((tm, o_sb.shape[1]), row),
            pl.BlockSpec((tm, o_mla.shape[1]), row),
            pl.BlockSpec((tm, o_mem.shape[1]), row),
            pl.BlockSpec((w_sb_o.shape[0], tn), colw),
            pl.BlockSpec((w_mla_o.shape[0], tn), colw),
            pl.BlockSpec((w_mem_o.shape[0], tn), colw),
            pl.BlockSpec((tm, tn), lambda i, j: (i, j)),
            pl.BlockSpec((tm, tn), lambda i, j: (i, nj + j)),
            pl.BlockSpec((tm, tn), lambda i, j: (i, 2 * nj + j)),
        ],
        out_specs=pl.BlockSpec((tm, tn), lambda i, j: (i, j)),
        out_shape=jax.ShapeDtypeStruct((m, d), BF16),
        compiler_params=_params("parallel", "parallel"),
        name="gated_merge",
    )(o_sb, o_mla, o_mem, w_sb_o, w_mla_o, w_mem_o, gates, gates, gates)


def _mm_kacc_kernel(a_ref, b_ref, r_ref, o_ref):
    @pl.when(pl.program_id(2) == 0)
    def _():
        o_ref[...] = r_ref[...]

    o_ref[...] += jnp.dot(a_ref[...], b_ref[...], preferred_element_type=F32)


def _matmul_kacc_residual(a, b, r, name):
    m, k = a.shape
    n = b.shape[1]
    tm, tn, tk = _tile(m, 1024), _tile(n, 1024), _tile(k, 4096)
    return pl.pallas_call(
        _mm_kacc_kernel,
        grid=(m // tm, n // tn, k // tk),
        in_specs=[
            pl.BlockSpec((tm, tk), lambda i, j, kk: (i, kk)),
            pl.BlockSpec((tk, tn), lambda i, j, kk: (kk, j)),
            pl.BlockSpec((tm, tn), lambda i, j, kk: (i, j)),
        ],
        out_specs=pl.BlockSpec((tm, tn), lambda i, j, kk: (i, j)),
        out_shape=jax.ShapeDtypeStruct((m, n), F32),
        compiler_params=_params("parallel", "parallel", "arbitrary"),
        name=name,
    )(a, b, r)


def _rope_tables():
    half = ROPE_DIM // 2
    freqs = 1.0 / (ROPE_THETA ** (jnp.arange(half, dtype=F32) / half))
    zeros = jnp.zeros((LANES - ROPE_DIM,), F32)
    freq = jnp.concatenate([freqs, freqs, zeros]).reshape(1, LANES)
    cmask = jnp.concatenate([jnp.ones((ROPE_DIM,), F32), zeros]).reshape(1, LANES)
    smask = jnp.concatenate([-jnp.ones((half,), F32), jnp.ones((half,), F32), zeros]).reshape(1, LANES)
    return freq, cmask, smask


def _pad_gain(g):
    return jnp.concatenate([g, jnp.zeros((LANES - g.shape[0],), g.dtype)]).reshape(1, LANES)


def _layer(x, mem, positions, g_mix, g_mem, w_in, g_cq, g_ckv, w_q_b, w_kv_b, g_q_mla, g_k_mla, w_mem_kv,
           g_q_mem, g_k_mem, w_sb_o, w_mla_o, w_mem_o, w_out, g_ffn, w_ff1, w_ff2):
    s, d = x.shape
    pos = positions.reshape(s, 1)

    c0 = 3 * SB_W
    c1 = c0 + Q_LORA
    c2 = c1 + KV_LORA
    c3 = c2 + ROPE_DIM
    c4 = c3 + MEM_W
    w_in_t = jnp.transpose(w_in)
    w_t = _cast_bf16(w_in_t, "cast_w_in", rows=c4)
    half = ROPE_DIM // 2
    wq = w_q_b.reshape(Q_LORA, MLA_HEADS, QK_DIM)
    wq_pad = jnp.zeros((Q_LORA, MLA_HEADS, LANES - ROPE_DIM), wq.dtype)
    w_qb = jnp.concatenate([wq, wq_pad, wq[..., NOPE_DIM + half:], wq[..., NOPE_DIM:NOPE_DIM + half], wq_pad], axis=-1)
    w_qb = w_qb.reshape(Q_LORA, MLA_HEADS * 3 * LANES).astype(BF16)
    w_kvb = w_kv_b.astype(BF16)

    bound = np.float32(np.sqrt(QK_DIM) * np.log2(np.e)) * jnp.max(jnp.abs(g_q_mla)) * jnp.max(jnp.abs(g_k_mla))
    shifted = bound <= MLA_SHIFT_BOUND
    shift_lane = (jnp.arange(LANES) == ROPE_DIM).astype(F32).reshape(1, LANES)
    q_bias = shift_lane * jnp.where(shifted, -bound, 0.0)
    k_bias = shift_lane

    h = _rmsnorm(x, g_mix, "rmsnorm_mix")

    tm = _tile(s, 1024)

    sb_scale = np.float32(-np.log2(np.e) / np.sqrt(SB_HEAD_DIM))
    colscale = jnp.concatenate([jnp.full((SB_W,), sb_scale, F32), jnp.ones((2 * SB_W,), F32)]).reshape(1, c0)
    tn = _tile(c0, 768)
    plan = _side_cast(w_in_t, (s // tm) * (c0 // tn), row0=c4, nrows=3 * d)
    sb_qkv, *casted = _matmul(
        h, w_t, b_transposed=True, n=c0, tm=tm, tn=tn, cast=[plan] if plan else [],
        extra=[(colscale, (1, tn), lambda i, j: (0, j))],
        outs=[((s, c0), BF16, (tm, tn), lambda i, j: (i, j))],
        epilogue=_ep_colscale, name="proj_sb_qkv")
    w_gate_t = casted[0] if plan else w_in_t[c4:].astype(BF16)
    o_sb = _sb_attention(sb_qkv)

    freq, cmask, smask = _rope_tables()
    tml = _tile(s, 512)
    nlat = c3 + LANES - ROPE_DIM - c0
    full = lambda shape: (shape, lambda i, j: (0, 0))
    rows = lambda w: ((tml, w), lambda i, j: (i, 0))
    plans = [_side_cast(w_out, s // tml), _side_cast(w_mem_kv, s // tml)]
    plans = plans if all(plans) else []
    cq_n, ckv_n, krot, cos_t, sin_t, *casted = _matmul(
        h, w_t, b_transposed=True, b_col0=c0, n=nlat, tm=tml, tn=nlat, cast=plans,
        extra=[(pos, *rows(1)), (freq, *full((1, LANES))), (cmask, *full((1, LANES))), (smask, *full((1, LANES))),
               (g_cq.reshape(1, Q_LORA), *full((1, Q_LORA))), (g_ckv.reshape(1, KV_LORA), *full((1, KV_LORA)))],
        outs=[((s, Q_LORA), BF16, *rows(Q_LORA)), ((s, KV_LORA), BF16, *rows(KV_LORA)),
              ((s, LANES), F32, *rows(LANES)), ((s, LANES), F32, *rows(LANES)), ((s, LANES), F32, *rows(LANES))],
        epilogue=_ep_latent, name="proj_mla_latent")
    w_out_b, w_mem_kv_b = casted if plans else (w_out.astype(BF16), w_mem_kv.astype(BF16))

    gq_n, gq_r = g_q_mla[:NOPE_DIM].reshape(1, LANES), _pad_gain(g_q_mla[NOPE_DIM:])
    gk_n, gk_r = g_k_mla[:NOPE_DIM].reshape(1, LANES), _pad_gain(g_k_mla[NOPE_DIM:])
    rows = lambda w: ((tm, w), lambda i, j: (i, 0))
    mla_scale = np.float32(np.log2(np.e) / np.sqrt(QK_DIM))
    tnh = MLA_HEADS_PER_STEP * 2 * LANES
    (q_mla,) = _matmul(
        cq_n, w_qb, tm=tm, tn=MLA_HEADS_PER_STEP * 3 * LANES,
        extra=[(cos_t, *rows(LANES)), (sin_t, *rows(LANES)), (gq_n, *full((1, LANES))), (gq_r, *full((1, LANES))),
               (q_bias, *full((1, LANES)))],
        outs=[((s, MLA_HEADS * 2 * LANES), BF16, (tm, tnh), lambda i, j: (i, j))],
        epilogue=functools.partial(_ep_mla_q, mla_scale), name="mla_q_up")
    k_mla, v_mla = _matmul(
        ckv_n, w_kvb, tm=tm, tn=tnh,
        extra=[(krot, *rows(LANES)), (gk_n, *full((1, LANES))), (gk_r, *full((1, LANES))),
               (k_bias, *full((1, LANES)))],
        outs=[((s, MLA_HEADS * 2 * LANES), BF16, (tm, tnh), lambda i, j: (i, j)),
              ((s, MLA_HEADS * 2 * LANES), BF16, (tm, tnh), lambda i, j: (i, j))],
        epilogue=_ep_mla_kv, name="mla_kv_up")
    o_mla = _mla_attention(shifted.astype(jnp.int32).reshape(1), q_mla, k_mla, v_mla)

    mkv = _memkv(mem, g_mem, w_mem_kv_b, g_k_mem)
    n_mem = mem.shape[0]
    tmm = _tile(s, 512)
    branch_w = [w_sb_o, w_mla_o, w_mem_o]
    plans = [_side_cast(w, s // tmm) for w in branch_w]
    plans = plans if all(plans) else []
    o_mem, *casted = _matmul(
        h, w_t, b_transposed=True, b_col0=c3, n=MEM_W, tm=tmm, tn=MEM_W, cast=plans,
        extra=[(mkv, (n_mem, MEM_W), lambda i, j: (0, 0)), (mkv, (n_mem, MEM_W), lambda i, j: (0, 1)),
               (g_q_mem.reshape(1, MEM_HEAD_DIM), (1, MEM_HEAD_DIM), lambda i, j: (0, 0))],
        outs=[((s, MEM_W), BF16, (tmm, MEM_W), lambda i, j: (i, 0))],
        epilogue=_ep_mem_attn, name="mem_attention")
    w_sb_o_b, w_mla_o_b, w_mem_o_b = casted if plans else [w.astype(BF16) for w in branch_w]

    tng = _tile(3 * d, 1024)
    g_col = g_ffn.reshape(d, 1)
    plan = _side_cast(w_ff1, (s // tm) * (3 * d // tng), scale=g_col)
    gates, *casted = _matmul(
        h, w_gate_t, b_transposed=True, tm=tm, tn=tng, cast=[plan] if plan else [],
        outs=[((s, 3 * d), BF16, (tm, tng), lambda i, j: (i, j))],
        epilogue=_ep_sigmoid, name="proj_gates")
    w_ff1_g = casted[0] if plan else (w_ff1 * g_col).astype(BF16)
    merged = _merge(o_sb, o_mla, o_mem, w_sb_o_b, w_mla_o_b, w_mem_o_b, gates)
    tnd = _tile(d, 512)
    x1, x1_b, x1_ss = _matmul(
        merged, w_out_b, tm=tm, tn=tnd, semantics=("parallel", "arbitrary"),
        extra=[(x, (tm, tnd), lambda i, j: (i, j))],
        outs=[((s, d), F32, (tm, tnd), lambda i, j: (i, j)), ((s, d), BF16, (tm, tnd), lambda i, j: (i, j)),
              ((s, LANES), F32, (tm, LANES), lambda i, j: (i, 0))],
        epilogue=_ep_residual_stats, name="out_proj")

    dff = w_ff1.shape[1]
    tnf = _tile(dff, 1024)
    plan = _side_cast(w_ff2, (s // tm) * (dff // tnf))
    u2, *casted = _matmul(
        x1_b, w_ff1_g, tm=tm, tn=tnf, cast=[plan] if plan else [],
        extra=[(x1_ss, (tm, LANES), lambda i, j: (i, 0))],
        outs=[((s, dff), BF16, (tm, tnf), lambda i, j: (i, j))],
        epilogue=functools.partial(_ep_relu2_rms, d), name="ffn_up")
    w_ff2_b = casted[0] if plan else w_ff2.astype(BF16)
    return _matmul_kacc_residual(u2, w_ff2_b, x1, "ffn_down")


def kernel(x, mem, positions, g_mix, g_mem, w_in, g_cq, g_ckv, w_q_b, w_kv_b, g_q_mla, g_k_mla, w_mem_kv, g_q_mem,
           g_k_mem, w_sb_o, w_mla_o, w_mem_o, w_out, g_ffn, w_ff1, w_ff2):
    depth = w_in.shape[0]
    assert x.shape[0] == 1 and mem.shape[0] == 1
    y = x[0]
    for i in range(depth):
        y = _layer(y, mem[0], positions[0], g_mix[i], g_mem[i], w_in[i], g_cq[i], g_ckv[i], w_q_b[i], w_kv_b[i],
                   g_q_mla[i], g_k_mla[i], w_mem_kv[i], g_q_mem[i], g_k_mem[i], w_sb_o[i], w_mla_o[i], w_mem_o[i],
                   w_out[i], g_ffn[i], w_ff1[i], w_ff2[i])
    return y[None]
```
